```python
import math
import jax, jax.numpy as jnp
from jax import lax
import numpy as np

D_MODEL = 1024
BATCH = 2
SEQ = 8192
DEPTH = 2
DEC_BATCH = 32
DEC_SEQ = 2048
PAST_LEN = 128

HEAD_DIM = 64
GRID_W = 64
ROPE_THETA = 10000.0
EPS = 1e-6
Q_BLOCK = 128
A_HEADS = 8
A_KV_HEADS = 2
B_HEADS = 4
C_GROUPS = ((128, 1), (512, 4), (2048, 16))
C_HEADS = 4
NUM_BUCKETS = 32
MAX_DISTANCE = 128
N_BIAS_HEADS = B_HEADS + len(C_GROUPS) * C_HEADS
D_FF = 2816
N_EXPERTS = 8
TOP_K = 2
D_FF_EXPERT = 2816
N_DENSE = (DEPTH + 1) // 2
N_MOE = DEPTH // 2
A_Q = A_HEADS * HEAD_DIM
A_KV = A_KV_HEADS * HEAD_DIM
B_QK = B_HEADS * 2 * HEAD_DIM
B_V = B_HEADS * 2 * HEAD_DIM
C_QKV = len(C_GROUPS) * C_HEADS * HEAD_DIM
C_OUT = C_HEADS * HEAD_DIM
N_BRANCH = 3
SPLIT_SIZES = (A_Q, A_KV, A_KV, B_QK, B_QK, B_V, C_QKV, C_QKV, C_QKV, N_BRANCH * D_MODEL)
SPLIT_AT = tuple(int(i) for i in np.cumsum(SPLIT_SIZES)[:-1])
IN_COLS = sum(SPLIT_SIZES)
BR_ROWS = A_Q + B_V + C_OUT

kernel_name = "hybrid_gated_bidir_encoder"


def rmsnorm(x, g):
    xf = x.astype(jnp.float32)
    y = xf * lax.rsqrt(jnp.mean(xf * xf, axis=-1, keepdims=True) + EPS)
    return (y * g.astype(jnp.float32)).astype(x.dtype)


def t5_bucket(rel):
    nb = NUM_BUCKETS // 2
    max_exact = nb // 2
    ret = jnp.where(rel > 0, nb, 0)
    n = jnp.abs(rel)
    large = max_exact + (jnp.log(jnp.maximum(n, max_exact).astype(jnp.float32) / max_exact)
                         / math.log(MAX_DISTANCE / max_exact) * (nb - max_exact)).astype(jnp.int32)
    large = jnp.minimum(large, nb - 1)
    return ret + jnp.where(n < max_exact, n, large)


def axial_rope_tables(s_len):
    rows = s_len // GRID_W
    row = jnp.repeat(jnp.arange(rows), GRID_W).astype(jnp.float32)
    col = jnp.tile(jnp.arange(GRID_W), rows).astype(jnp.float32)
    half = HEAD_DIM // 2
    inv = ROPE_THETA ** (-jnp.arange(0, half, 2, dtype=jnp.float32) / half)
    ang = jnp.stack([row[:, None] * inv, col[:, None] * inv], axis=1)
    return jnp.cos(ang)[:, None], jnp.sin(ang)[:, None]


def apply_axial_rope(x, cos, sin):
    xf = x.astype(jnp.float32).reshape(*x.shape[:-1], 2, 2, HEAD_DIM // 4)
    x1, x2 = xf[..., 0, :], xf[..., 1, :]
    out = jnp.stack([x1 * cos - x2 * sin, x2 * cos + x1 * sin], axis=-2)
    return out.reshape(x.shape).astype(x.dtype)


def mixer_a(q, k, v, cos, sin, qn, kn):
    bsz, s_len = q.shape[:2]
    q = apply_axial_rope(rmsnorm(q, qn), cos, sin)
    k = apply_axial_rope(rmsnorm(k, kn), cos, sin)
    grp = A_HEADS // A_KV_HEADS
    nblk = s_len // Q_BLOCK
    qb = q.reshape(bsz, nblk, Q_BLOCK, A_KV_HEADS, grp, HEAD_DIM).transpose(1, 0, 2, 3, 4, 5)
    scale = HEAD_DIM ** -0.5

    def block(qi):
        s = jnp.einsum('bqkgd,bskd->bkgqs', qi, k, preferred_element_type=jnp.float32) * scale
        p = jax.nn.softmax(s, axis=-1).astype(v.dtype)
        return jnp.einsum('bkgqs,bskd->bqkgd', p, v)

    o = lax.map(block, qb)
    return o.transpose(1, 0, 2, 3, 4, 5).reshape(bsz, s_len, A_Q)


def mixer_b(q, k, v, bias_cols, lq1, lk1, lq2, lk2, subln, lambda_init):
    bsz, s_len = q.shape[:2]
    lam = (jnp.exp(jnp.sum(lq1.astype(jnp.float32) * lk1.astype(jnp.float32)))
           - jnp.exp(jnp.sum(lq2.astype(jnp.float32) * lk2.astype(jnp.float32))) + lambda_init)
    nblk = s_len // Q_BLOCK
    qb = q.reshape(bsz, nblk, Q_BLOCK, B_HEADS, 2, HEAD_DIM).transpose(1, 0, 2, 3, 4, 5)
    kpos = jnp.arange(s_len)
    scale = HEAD_DIM ** -0.5

    def block(args):
        qi, i = args
        s = jnp.einsum('bqhmd,bshmd->bhmqs', qi, k, preferred_element_type=jnp.float32) * scale
        qpos = i * Q_BLOCK + jnp.arange(Q_BLOCK)
        bias = bias_cols[t5_bucket(kpos[None, :] - qpos[:, None])].astype(jnp.float32)
        s = s + bias.transpose(2, 0, 1)[None, :, None]
        p = jax.nn.softmax(s, axis=-1)
        a = p[:, :, 0] - lam * p[:, :, 1]
        return jnp.einsum('bhqs,bshe->bqhe', a.astype(v.dtype), v)

    o = lax.map(block, (qb, jnp.arange(nblk)))
    o = o.transpose(1, 0, 2, 3, 4).reshape(bsz, s_len, B_HEADS, 2 * HEAD_DIM)
    o = rmsnorm(o, subln) * (1.0 - lambda_init)
    return o.reshape(bsz, s_len, B_V)


def dilated_window_group(q, k, v, window, dil, bias_cols):
    bsz, s_len, nh, hd = q.shape
    span = window // (2 * dil)
    L = s_len // dil
    nb = -(-L // span)
    Lp = nb * span

    def by_residue(x):
        return x.reshape(bsz, L, dil, nh, hd).transpose(0, 2, 1, 3, 4)

    qr, kr, vr = by_residue(q), by_residue(k), by_residue(v)
    qr = jnp.pad(qr, ((0, 0), (0, 0), (0, Lp - L), (0, 0), (0, 0)))
    kp = jnp.pad(kr, ((0, 0), (0, 0), (span, Lp - L + span), (0, 0), (0, 0)))
    vp = jnp.pad(vr, ((0, 0), (0, 0), (span, Lp - L + span), (0, 0), (0, 0)))
    qb = qr.reshape(bsz, dil, nb, span, nh, hd)

    def band(x):
        xb = x.reshape(bsz, dil, nb + 2, span, nh, hd)
        return jnp.concatenate([xb[:, :, :-2], xb[:, :, 1:-1], xb[:, :, 2:]], axis=3)

    kb, vb = band(kp), band(vp)
    iq = jnp.arange(span)
    jk = jnp.arange(3 * span)
    rel = jk[None, :] - span - iq[:, None]
    ukey = jnp.arange(nb)[:, None] * span - span + jk[None, :]
    mask = (jnp.abs(rel) <= span)[None] & ((ukey >= 0) & (ukey < L))[:, None, :]
    bias = bias_cols[t5_bucket(rel * dil)].astype(jnp.float32).transpose(2, 0, 1)
    s = jnp.einsum('brnqhd,brnkhd->brnhqk', qb, kb, preferred_element_type=jnp.float32) * (hd ** -0.5) + bias
    s = jnp.where(mask[:, None], s, -jnp.inf)
    m = jnp.max(s, axis=-1, keepdims=True)
    e = jnp.exp(s - m)
    den = jnp.sum(e, axis=-1, keepdims=True)
    o = jnp.einsum('brnhqk,brnkhd->brnqhd', (e / den).astype(v.dtype), vb)
    lse = (m + jnp.log(den))[..., 0]
    o = o.reshape(bsz, dil, Lp, nh, hd)[:, :, :L].transpose(0, 2, 1, 3, 4).reshape(bsz, s_len, nh, hd)
    lse = lse.transpose(0, 1, 2, 4, 3).reshape(bsz, dil, Lp, nh)[:, :, :L].transpose(0, 2, 1, 3).reshape(bsz, s_len, nh)
    return o, lse


def mixer_c(q, k, v, bias_cols):
    bsz, s_len = q.shape[:2]
    outs, lses = [], []
    for g, (window, dil) in enumerate(C_GROUPS):
        o, lse = dilated_window_group(q[:, :, g], k[:, :, g], v[:, :, g], window, dil,
                                      bias_cols[:, g * C_HEADS:(g + 1) * C_HEADS])
        outs.append(o)
        lses.append(lse)
    w = jax.nn.softmax(jnp.stack(lses, axis=0), axis=0)
    o = jnp.sum(w[..., None] * jnp.stack(outs, axis=0).astype(jnp.float32), axis=0)
    return o.astype(q.dtype).reshape(bsz, s_len, C_OUT)


def swiglu(h, w1, w3, w2):
    return (jax.nn.silu(h @ w1) * (h @ w3)) @ w2


def moe_swiglu(h, router, e_w1, e_w3, e_w2):
    logits = jnp.einsum('bsd,de->bse', h, router, preferred_element_type=jnp.float32)
    top_v, top_i = lax.top_k(logits, TOP_K)
    top_w = jax.nn.softmax(top_v, axis=-1)
    gate = jnp.sum(jax.nn.one_hot(top_i, N_EXPERTS, dtype=jnp.float32) * top_w[..., None], axis=-2)
    gate = gate.astype(h.dtype)
    out = jnp.zeros_like(h)
    for e in range(N_EXPERTS):
        out = out + gate[..., e:e + 1] * swiglu(h, e_w1[e], e_w3[e], e_w2[e])
    return out


def encoder_trunk(x, rel_bias, ln1, ln2, ln_f, w_in, a_qnorm, a_knorm, lam_q1, lam_k1, lam_q2, lam_k2,
                  b_subln, w_br, w_o, ffn_w1, ffn_w3, ffn_w2, router, exp_w1, exp_w3, exp_w2):
    bsz, s_len, _ = x.shape
    cos, sin = axial_rope_tables(s_len)
    n_groups = len(C_GROUPS)
    for l in range(DEPTH):
        h = rmsnorm(x, ln1[l])
        a_q, a_k, a_v, b_q, b_k, b_v, c_q, c_k, c_v, g = jnp.split(h @ w_in[l], SPLIT_AT, axis=-1)
        o_a = mixer_a(a_q.reshape(bsz, s_len, A_HEADS, HEAD_DIM),
                      a_k.reshape(bsz, s_len, A_KV_HEADS, HEAD_DIM),
                      a_v.reshape(bsz, s_len, A_KV_HEADS, HEAD_DIM),
                      cos, sin, a_qnorm[l], a_knorm[l])
        lambda_init = 0.8 - 0.6 * math.exp(-0.3 * l)
        o_b = mixer_b(b_q.reshape(bsz, s_len, B_HEADS, 2, HEAD_DIM),
                      b_k.reshape(bsz, s_len, B_HEADS, 2, HEAD_DIM),
                      b_v.reshape(bsz, s_len, B_HEADS, 2 * HEAD_DIM),
                      rel_bias[:, :B_HEADS], lam_q1[l], lam_k1[l], lam_q2[l], lam_k2[l], b_subln[l], lambda_init)
        o_c = mixer_c(c_q.reshape(bsz, s_len, n_groups, C_HEADS, HEAD_DIM),
                      c_k.reshape(bsz, s_len, n_groups, C_HEADS, HEAD_DIM),
                      c_v.reshape(bsz, s_len, n_groups, C_HEADS, HEAD_DIM),
                      rel_bias[:, B_HEADS:])
        gates = jax.nn.sigmoid(g.reshape(bsz, s_len, N_BRANCH, D_MODEL).astype(jnp.float32)).astype(x.dtype)
        wb = w_br[l]
        merged = (gates[:, :, 0] * (o_a @ wb[:A_Q])
                  + gates[:, :, 1] * (o_b @ wb[A_Q:A_Q + B_V])
                  + gates[:, :, 2] * (o_c @ wb[A_Q + B_V:]))
        x = x + merged @ w_o[l]
        h = rmsnorm(x, ln2[l])
        j = l // 2
        if l % 2 == 0:
            x = x + swiglu(h, ffn_w1[j], ffn_w3[j], ffn_w2[j])
        else:
            x = x + moe_swiglu(h, router[j], exp_w1[j], exp_w3[j], exp_w2[j])
    return rmsnorm(x, ln_f)


def setup_inputs(seed: int = 0) -> dict:
    key = jax.random.key(seed)
    ks = jax.random.split(key, 26)
    f32 = jnp.float32

    def nrm(k, shape, scale):
        return jax.random.normal(k, shape, f32) * scale

    def gain(k, shape):
        return 1.0 + 0.02 * jax.random.normal(k, shape, f32)

    w_br = jnp.concatenate([nrm(ks[10], (DEPTH, A_Q, D_MODEL), A_Q ** -0.5),
                            nrm(ks[11], (DEPTH, B_V, D_MODEL), B_V ** -0.5),
                            nrm(ks[12], (DEPTH, C_OUT, D_MODEL), C_OUT ** -0.5)], axis=1)
    return {
        "x_prompt": jax.random.normal(ks[0], (BATCH, SEQ, D_MODEL), f32),
        "x_sample": jax.random.normal(ks[1], (DEC_BATCH, DEC_SEQ, D_MODEL), f32),
        "rel_bias": nrm(ks[2], (NUM_BUCKETS, N_BIAS_HEADS), 0.5),
        "ln1": gain(ks[3], (DEPTH, D_MODEL)),
        "ln2": gain(ks[4], (DEPTH, D_MODEL)),
        "ln_f": gain(ks[5], (D_MODEL,)),
        "w_in": nrm(ks[6], (DEPTH, D_MODEL, IN_COLS), D_MODEL ** -0.5),
        "a_qnorm": gain(ks[7], (DEPTH, HEAD_DIM)),
        "a_knorm": gain(ks[8], (DEPTH, HEAD_DIM)),
        "lam_q1": nrm(ks[9], (DEPTH, HEAD_DIM), 0.1),
        "lam_k1": nrm(ks[13], (DEPTH, HEAD_DIM), 0.1),
        "lam_q2": nrm(ks[14], (DEPTH, HEAD_DIM), 0.1),
        "lam_k2": nrm(ks[15], (DEPTH, HEAD_DIM), 0.1),
        "b_subln": gain(ks[16], (DEPTH, 2 * HEAD_DIM)),
        "w_br": w_br,
        "w_o": nrm(ks[17], (DEPTH, D_MODEL, D_MODEL), D_MODEL ** -0.5),
        "ffn_w1": nrm(ks[18], (N_DENSE, D_MODEL, D_FF), D_MODEL ** -0.5),
        "ffn_w3": nrm(ks[19], (N_DENSE, D_MODEL, D_FF), D_MODEL ** -0.5),
        "ffn_w2": nrm(ks[20], (N_DENSE, D_FF, D_MODEL), D_FF ** -0.5),
        "router": nrm(ks[21], (N_MOE, D_MODEL, N_EXPERTS), D_MODEL ** -0.5),
        "exp_w1": nrm(ks[22], (N_MOE, N_EXPERTS, D_MODEL, D_FF_EXPERT), D_MODEL ** -0.5),
        "exp_w3": nrm(ks[23], (N_MOE, N_EXPERTS, D_MODEL, D_FF_EXPERT), D_MODEL ** -0.5),
        "exp_w2": nrm(ks[24], (N_MOE, N_EXPERTS, D_FF_EXPERT, D_MODEL), D_FF_EXPERT ** -0.5),
    }


def reference(x_prompt, x_sample, rel_bias, ln1, ln2, ln_f, w_in, a_qnorm, a_knorm, lam_q1, lam_k1,
              lam_q2, lam_k2, b_subln, w_br, w_o, ffn_w1, ffn_w3, ffn_w2, router, exp_w1, exp_w3, exp_w2):
    y_prompt = encoder_trunk(x_prompt, rel_bias, ln1, ln2, ln_f, w_in, a_qnorm, a_knorm, lam_q1, lam_k1,
                             lam_q2, lam_k2, b_subln, w_br, w_o, ffn_w1, ffn_w3, ffn_w2, router,
                             exp_w1, exp_w3, exp_w2)
    y_sample = encoder_trunk(x_sample, rel_bias, ln1, ln2, ln_f, w_in, a_qnorm, a_knorm, lam_q1, lam_k1,
                             lam_q2, lam_k2, b_subln, w_br, w_o, ffn_w1, ffn_w3, ffn_w2, router,
                             exp_w1, exp_w3, exp_w2)
    return (y_prompt, y_sample)
```

```python
import functools
import math

import jax
import jax.numpy as jnp
import numpy as np
from jax import lax
from jax.experimental import pallas as pl
from jax.experimental.pallas import tpu as pltpu

F32 = jnp.float32
BF16 = jnp.bfloat16

D_MODEL = 1024
HEAD_DIM = 64
GRID_W = 64
ROPE_THETA = 10000.0
EPS = 1e-6
A_HEADS = 8
A_KV_HEADS = 2
B_HEADS = 4
C_GROUPS = ((128, 1), (512, 4), (2048, 16))
C_HEADS = 4
NUM_BUCKETS = 32
MAX_DISTANCE = 128
N_EXPERTS = 8
TOP_K = 2
N_BRANCH = 3

A_Q = A_HEADS * HEAD_DIM
A_KV = A_KV_HEADS * HEAD_DIM
B_QK = B_HEADS * 2 * HEAD_DIM
B_V = B_HEADS * 2 * HEAD_DIM
C_QKV = len(C_GROUPS) * C_HEADS * HEAD_DIM
C_OUT = C_HEADS * HEAD_DIM
GATE_COLS = N_BRANCH * D_MODEL
IN_COLS = A_Q + 2 * A_KV + 2 * B_QK + B_V + 3 * C_QKV + GATE_COLS

OFF_GATE = 0
OFF_AQ = GATE_COLS
OFF_AK = OFF_AQ + A_Q
OFF_AV = OFF_AK + A_KV
OFF_BQ = OFF_AV + A_KV
OFF_BK = OFF_BQ + B_QK
OFF_BV = OFF_BK + B_QK
OFF_CQ = OFF_BV + B_V
OFF_CK = OFF_CQ + C_QKV
OFF_CV = OFF_CK + C_QKV

LANES = 128
VMEM_LIMIT = 56 * 1024 * 1024
LOG2E = 1.4426950408889634
LN2 = 0.6931471805599453
QK_SCALE = HEAD_DIM ** -0.5 * LOG2E
NEG_BIG = -1e30

NT_DIMS = (((1,), (1,)), ((), ()))


def _cparams(sem):
    return pltpu.CompilerParams(dimension_semantics=sem, vmem_limit_bytes=VMEM_LIMIT)


def _norm_proj_kernel(x_ref, g_ref, w_ref, cs_ref, o_ref, h_ref, *, n_gate_tiles):
    j = pl.program_id(1)

    @pl.when(j == 0)
    def _():
        x = x_ref[...]
        ms = jnp.mean(x * x, axis=-1, keepdims=True)
        h_ref[...] = (x * lax.rsqrt(ms + EPS) * g_ref[...]).astype(BF16)

    y = jnp.dot(h_ref[...], w_ref[...], preferred_element_type=F32) * cs_ref[...]

    @pl.when(j < n_gate_tiles)
    def _():
        o_ref[...] = jax.nn.sigmoid(y).astype(BF16)

    @pl.when(j >= n_gate_tiles)
    def _():
        o_ref[...] = y.astype(BF16)


def _norm_proj(x, gain, w, colscale, *, tm=512, tn=768):
    n, d = x.shape
    cols = w.shape[1]
    assert n % tm == 0 and cols % tn == 0 and GATE_COLS % tn == 0
    return pl.pallas_call(
        functools.partial(_norm_proj_kernel, n_gate_tiles=GATE_COLS // tn),
        grid=(n // tm, cols // tn),
        in_specs=[
            pl.BlockSpec((tm, d), lambda i, j: (i, 0)),
            pl.BlockSpec((1, d), lambda i, j: (0, 0)),
            pl.BlockSpec((d, tn), lambda i, j: (0, j)),
            pl.BlockSpec((1, tn), lambda i, j: (0, j)),
        ],
        out_specs=pl.BlockSpec((tm, tn), lambda i, j: (i, j)),
        out_shape=jax.ShapeDtypeStruct((n, cols), BF16),
        scratch_shapes=[pltpu.VMEM((tm, d), BF16)],
        compiler_params=_cparams(("parallel", "arbitrary")),
        name="norm_proj",
    )(x, gain.reshape(1, d), w, colscale)


def _prep_a_kernel(y_ref, cos_ref, sin_ref, qg_ref, kg_ref, bd_ref, q_out, k_out, v_out):
    tm = y_ref.shape[0]
    lane = lax.broadcasted_iota(jnp.int32, (tm, LANES), 1)
    first16 = (lane % 32) < 16
    half0 = lane < HEAD_DIM
    cos = cos_ref[...]
    sin = sin_ref[...]
    bd = bd_ref[...]

    def norm_rope(xb, gain, out_scale):
        x = xb.astype(F32)
        ss = jnp.dot(x * x, bd, preferred_element_type=F32, precision=lax.Precision.HIGHEST)
        xn = x * lax.rsqrt(ss * (1.0 / HEAD_DIM) + EPS) * gain
        sw = jnp.where(first16, pltpu.roll(xn, LANES - 16, 1), pltpu.roll(xn, 16, 1))
        return (xn * cos + sw * sin) * out_scale

    qg = qg_ref[...]
    for c in range(A_Q // LANES):
        q_out[:, c * LANES:(c + 1) * LANES] = norm_rope(
            y_ref[:, c * LANES:(c + 1) * LANES], qg, QK_SCALE).astype(BF16)

    k = norm_rope(y_ref[:, A_Q:A_Q + LANES], kg_ref[...], 1.0)
    kr = pltpu.roll(k, HEAD_DIM, 1)
    k_out[:, 0:LANES] = jnp.where(half0, k, kr).astype(BF16)
    k_out[:, LANES:2 * LANES] = jnp.where(half0, kr, k).astype(BF16)

    v = y_ref[:, A_Q + LANES:A_Q + 2 * LANES].astype(F32)
    vr = pltpu.roll(v, HEAD_DIM, 1)
    v_out[:, 0:LANES] = jnp.where(half0, v, vr).astype(BF16)
    v_out[:, LANES:2 * LANES] = jnp.where(half0, vr, v).astype(BF16)


def _prep_a(y, cos_t, sin_t, qg, kg, bd, seq, *, tm=512):
    n = y.shape[0]
    width = A_Q + 2 * A_KV
    assert OFF_AQ % width == 0 and seq % tm == 0 and A_KV == LANES
    spt = seq // tm
    return pl.pallas_call(
        _prep_a_kernel,
        grid=(n // tm,),
        in_specs=[
            pl.BlockSpec((tm, width), lambda i: (i, OFF_AQ // width)),
            pl.BlockSpec((tm, LANES), lambda i: (i % spt, 0)),
            pl.BlockSpec((tm, LANES), lambda i: (i % spt, 0)),
            pl.BlockSpec((1, LANES), lambda i: (0, 0)),
            pl.BlockSpec((1, LANES), lambda i: (0, 0)),
            pl.BlockSpec((LANES, LANES), lambda i: (0, 0)),
        ],
        out_specs=[
            pl.BlockSpec((tm, A_Q), lambda i: (i, 0)),
            pl.BlockSpec((tm, 2 * LANES), lambda i: (i, 0)),
            pl.BlockSpec((tm, 2 * LANES), lambda i: (i, 0)),
        ],
        out_shape=[
            jax.ShapeDtypeStruct((n, A_Q), BF16),
            jax.ShapeDtypeStruct((n, 2 * LANES), BF16),
            jax.ShapeDtypeStruct((n, 2 * LANES), BF16),
        ],
        compiler_params=_cparams(("parallel",)),
        name="prep_a",
    )(y, cos_t, sin_t, qg, kg, bd)


def _flash_init(m_ref, l_ref, acc_ref):
    m_ref[...] = jnp.full(m_ref.shape, NEG_BIG, F32)
    l_ref[...] = jnp.zeros(l_ref.shape, F32)
    acc_ref[...] = jnp.zeros(acc_ref.shape, F32)


def _flash_update(g, tq, s, v, m_ref, l_ref, acc_ref):
    rows = pl.ds(g * tq, tq)
    m_prev = m_ref[rows, :]
    m_cur = jnp.maximum(m_prev, jnp.max(s, axis=-1, keepdims=True))
    alpha = jnp.exp2(m_prev - m_cur)
    p = jnp.exp2(s - m_cur[:, 0:1])
    l_ref[rows, :] = alpha * l_ref[rows, :] + jnp.sum(p, axis=-1, keepdims=True)
    acc_ref[rows, :] = alpha * acc_ref[rows, :] + jnp.dot(
        p.astype(BF16), v, preferred_element_type=F32)
    m_ref[rows, :] = m_cur


def _flash_a_kernel(q_ref, k_ref, v_ref, o_ref, qs_ref, m_ref, l_ref, acc_ref, *, tq, nk):
    ki = pl.program_id(3)
    grp = A_HEADS // A_KV_HEADS
    lane = lax.broadcasted_iota(jnp.int32, (tq, LANES), 1)
    half0 = lane < HEAD_DIM

    @pl.when(ki == 0)
    def _():
        _flash_init(m_ref, l_ref, acc_ref)
        for h in range(grp):
            c = h // 2
            x = q_ref[:, c * LANES:(c + 1) * LANES]
            keep = half0 if h % 2 == 0 else jnp.logical_not(half0)
            qs_ref[h * tq:(h + 1) * tq, :] = jnp.where(keep, x, jnp.zeros_like(x))

    k = k_ref[...]
    v = v_ref[...]
    for h in range(grp):
        s = lax.dot_general(qs_ref[h * tq:(h + 1) * tq, :], k, NT_DIMS, preferred_element_type=F32)
        _flash_update(h, tq, s, v, m_ref, l_ref, acc_ref)

    @pl.when(ki == nk - 1)
    def _():
        for c in range(grp // 2):
            r0 = pl.ds((2 * c) * tq, tq)
            r1 = pl.ds((2 * c + 1) * tq, tq)
            o0 = acc_ref[r0, :] / l_ref[r0, :]
            o1 = acc_ref[r1, :] / l_ref[r1, :]
            o_ref[:, c * LANES:(c + 1) * LANES] = jnp.where(half0, o0, o1).astype(BF16)


def _flash_a(qn, kd, vd, bsz, seq, *, tq=256, tk=1024):
    n = qn.shape[0]
    tk = min(tk, seq)
    assert seq % tq == 0 and seq % tk == 0
    nq, nk = seq // tq, seq // tk
    grp = A_HEADS // A_KV_HEADS
    qw = grp * HEAD_DIM
    return pl.pallas_call(
        functools.partial(_flash_a_kernel, tq=tq, nk=nk),
        grid=(bsz, A_KV_HEADS, nq, nk),
        in_specs=[
            pl.BlockSpec((tq, qw), lambda b, kv, qi, ki: (b * nq + qi, kv)),
            pl.BlockSpec((tk, LANES), lambda b, kv, qi, ki: (b * nk + ki, kv)),
            pl.BlockSpec((tk, LANES), lambda b, kv, qi, ki: (b * nk + ki, kv)),
        ],
        out_specs=pl.BlockSpec((tq, qw), lambda b, kv, qi, ki: (b * nq + qi, kv)),
        out_shape=jax.ShapeDtypeStruct((n, A_Q), BF16),
        scratch_shapes=[
            pltpu.VMEM((grp * tq, LANES), BF16),
            pltpu.VMEM((grp * tq, LANES), F32),
            pltpu.VMEM((grp * tq, LANES), F32),
            pltpu.VMEM((grp * tq, LANES), F32),
        ],
        compiler_params=_cparams(("parallel", "parallel", "parallel", "arbitrary")),
        name="flash_a",
    )(qn, kd, vd)


def _flash_b_kernel(lam_ref, q_ref, k_ref, v_ref, bias_ref, sub_ref, o_ref,
                    qs_ref, m_ref, l_ref, acc_ref, *, tq, nk, out_scale):
    qi = pl.program_id(2)
    ki = pl.program_id(3)
    lane = lax.broadcasted_iota(jnp.int32, (tq, LANES), 1)
    half0 = lane < HEAD_DIM

    @pl.when(ki == 0)
    def _():
        _flash_init(m_ref, l_ref, acc_ref)
        x = q_ref[...]
        zero = jnp.zeros_like(x)
        qs_ref[0:tq, :] = jnp.where(half0, x, zero)
        qs_ref[tq:2 * tq, :] = jnp.where(half0, zero, x)

    k = k_ref[...]
    v = v_ref[...]
    bias = bias_ref[0, jnp.clip(ki - qi, -2, 2) + 2]
    for mp in range(2):
        s = lax.dot_general(qs_ref[mp * tq:(mp + 1) * tq, :], k, NT_DIMS, preferred_element_type=F32)
        _flash_update(mp, tq, s + bias, v, m_ref, l_ref, acc_ref)

    @pl.when(ki == nk - 1)
    def _():
        lam = lam_ref[0]
        o = (acc_ref[0:tq, :] / l_ref[0:tq, :]
             - lam * (acc_ref[tq:2 * tq, :] / l_ref[tq:2 * tq, :]))
        ms = jnp.mean(o * o, axis=-1, keepdims=True)
        o_ref[...] = (o * lax.rsqrt(ms + EPS) * sub_ref[...] * out_scale).astype(BF16)


def _flash_b(y, bias_tiles, lam, subln, bsz, seq, lambda_init, *, t=512):
    n = y.shape[0]
    t = min(t, seq)
    assert seq % t == 0 and bias_tiles.shape == (B_HEADS, 5, t, t)
    nq = seq // t
    qb, kb, vb = OFF_BQ // LANES, OFF_BK // LANES, OFF_BV // LANES
    return pl.pallas_call(
        functools.partial(_flash_b_kernel, tq=t, nk=nq, out_scale=1.0 - lambda_init),
        grid=(B_HEADS, bsz, nq, nq),
        in_specs=[
            pl.BlockSpec(memory_space=pltpu.SMEM),
            pl.BlockSpec((t, LANES), lambda h, b, qi, ki: (b * nq + qi, qb + h)),
            pl.BlockSpec((t, LANES), lambda h, b, qi, ki: (b * nq + ki, kb + h)),
            pl.BlockSpec((t, LANES), lambda h, b, qi, ki: (b * nq + ki, vb + h)),
            pl.BlockSpec((1, 5, t, t), lambda h, b, qi, ki: (h, 0, 0, 0)),
            pl.BlockSpec((1, LANES), lambda h, b, qi, ki: (0, 0)),
        ],
        out_specs=pl.BlockSpec((t, LANES), lambda h, b, qi, ki: (b * nq + qi, h)),
        out_shape=jax.ShapeDtypeStruct((n, B_V), BF16),
        scratch_shapes=[
            pltpu.VMEM((2 * t, LANES), BF16),
            pltpu.VMEM((2 * t, LANES), F32),
            pltpu.VMEM((2 * t, LANES), F32),
            pltpu.VMEM((2 * t, LANES), F32),
        ],
        compiler_params=_cparams(("parallel", "parallel", "parallel", "arbitrary")),
        name="flash_b",
    )(lam, y, y, y, bias_tiles, subln)


def _window_kernel(q_ref, kp_ref, km_ref, kn_ref, vp_ref, vm_ref, vn_ref, bias_ref,
                   o_ref, lse_ref, *, tq, halo, length):
    i = pl.program_id(2)
    q = q_ref[0]
    kcat = jnp.concatenate([kp_ref[0], km_ref[0], kn_ref[0]], axis=0)
    vcat = jnp.concatenate([vp_ref[0], vm_ref[0], vn_ref[0]], axis=0)
    nkeys = tq + 2 * halo
    ukey = i * tq - halo + lax.broadcasted_iota(jnp.int32, (1, nkeys), 1)
    colmask = jnp.where((ukey >= 0) & (ukey < length), 0.0, NEG_BIG).astype(F32)
    lane = lax.broadcasted_iota(jnp.int32, (tq, LANES), 1)
    half0 = lane < HEAD_DIM
    for c in range(C_HEADS // 2):
        qc = q[:, c * LANES:(c + 1) * LANES]
        kc = kcat[:, c * LANES:(c + 1) * LANES]
        vc = vcat[:, c * LANES:(c + 1) * LANES]
        outs, lses = [], []
        for hh in range(2):
            keep = half0 if hh == 0 else jnp.logical_not(half0)
            qh = jnp.where(keep, qc, jnp.zeros_like(qc))
            s = lax.dot_general(qh, kc, NT_DIMS, preferred_element_type=F32)
            s = s + bias_ref[2 * c + hh] + colmask
            m = jnp.max(s, axis=-1, keepdims=True)
            e = jnp.exp2(s - m)
            den = jnp.sum(e, axis=-1, keepdims=True)
            outs.append(jnp.dot(e.astype(BF16), vc, preferred_element_type=F32) / den)
            lses.append(jnp.broadcast_to((m + jnp.log2(den)) * LN2, (tq, LANES)))
        o_ref[0, :, c * LANES:(c + 1) * LANES] = jnp.where(half0, outs[0], outs[1]).astype(BF16)
        lse_ref[0, :, c * LANES:(c + 1) * LANES] = jnp.where(half0, lses[0], lses[1])


def _window_group(y3, bias_tile, g, dil, bsz, seq, *, tq=128, halo=128):
    length = seq // dil
    cw = C_HEADS * HEAD_DIM
    assert length % tq == 0 and tq % halo == 0 and length % halo == 0
    assert IN_COLS % cw == 0 and OFF_CQ % cw == 0 and C_QKV % cw == 0
    per_res = IN_COLS // cw
    qc, kc, vc = OFF_CQ // cw + g, OFF_CK // cw + g, OFF_CV // cw + g
    nt = length // tq
    hb = tq // halo
    nhb = length // halo

    def main_map(col):
        return lambda b, r, i: (b, i, r * per_res + col)

    def prev_map(col):
        return lambda b, r, i: (b, jnp.maximum(i * hb - 1, 0), r * per_res + col)

    def next_map(col):
        return lambda b, r, i: (b, jnp.minimum((i + 1) * hb, nhb - 1), r * per_res + col)

    out_map = lambda b, r, i: (b, i, r)
    return pl.pallas_call(
        functools.partial(_window_kernel, tq=tq, halo=halo, length=length),
        grid=(bsz, dil, nt),
        in_specs=[
            pl.BlockSpec((1, tq, cw), main_map(qc)),
            pl.BlockSpec((1, halo, cw), prev_map(kc)),
            pl.BlockSpec((1, tq, cw), main_map(kc)),
            pl.BlockSpec((1, halo, cw), next_map(kc)),
            pl.BlockSpec((1, halo, cw), prev_map(vc)),
            pl.BlockSpec((1, tq, cw), main_map(vc)),
            pl.BlockSpec((1, halo, cw), next_map(vc)),
            pl.BlockSpec((C_HEADS, tq, tq + 2 * halo), lambda b, r, i: (0, 0, 0)),
        ],
        out_specs=[
            pl.BlockSpec((1, tq, cw), out_map),
            pl.BlockSpec((1, tq, cw), out_map),
        ],
        out_shape=[
            jax.ShapeDtypeStruct((bsz, length, dil * cw), BF16),
            jax.ShapeDtypeStruct((bsz, length, dil * cw), F32),
        ],
        compiler_params=_cparams(("parallel", "parallel", "parallel")),
        name=f"window_g{g}",
    )(y3, y3, y3, y3, y3, y3, y3, bias_tile)


def _merge_kernel(x_ref, oa_ref, ob_ref, oc0_ref, oc1_ref, oc2_ref, ls0_ref, ls1_ref, ls2_ref,
                  g0_ref, g1_ref, g2_ref, wb_ref, wo_ref, o_ref):
    ls = [ls0_ref[...], ls1_ref[...], ls2_ref[...]]
    ocs = [oc0_ref[...], oc1_ref[...], oc2_ref[...]]
    m = jnp.maximum(jnp.maximum(ls[0], ls[1]), ls[2])
    es = [jnp.exp(l - m) for l in ls]
    den = es[0] + es[1] + es[2]
    oc = (es[0] * ocs[0].astype(F32) + es[1] * ocs[1].astype(F32) + es[2] * ocs[2].astype(F32)) / den
    ya = jnp.dot(oa_ref[...], wb_ref[0:A_Q, :], preferred_element_type=F32)
    yb = jnp.dot(ob_ref[...], wb_ref[A_Q:A_Q + B_V, :], preferred_element_type=F32)
    yc = jnp.dot(oc.astype(BF16), wb_ref[A_Q + B_V:, :], preferred_element_type=F32)
    merged = (g0_ref[...].astype(F32) * ya + g1_ref[...].astype(F32) * yb
              + g2_ref[...].astype(F32) * yc)
    o_ref[...] = x_ref[...] + jnp.dot(merged.astype(BF16), wo_ref[...], preferred_element_type=F32)


def _merge(x, oa, ob, ocs, lss, y, wb, wo, *, tm=512):
    n, d = x.shape
    assert n % tm == 0 and OFF_GATE == 0
    row = lambda w: pl.BlockSpec((tm, w), lambda i: (i, 0))
    gate = lambda j: pl.BlockSpec((tm, d), lambda i: (i, j))
    return pl.pallas_call(
        _merge_kernel,
        grid=(n // tm,),
        in_specs=[row(d), row(A_Q), row(B_V)] + [row(C_OUT)] * 6 + [gate(0), gate(1), gate(2)] + [
            pl.BlockSpec(wb.shape, lambda i: (0, 0)),
            pl.BlockSpec(wo.shape, lambda i: (0, 0)),
        ],
        out_specs=row(d),
        out_shape=jax.ShapeDtypeStruct((n, d), F32),
        compiler_params=_cparams(("parallel",)),
        name="merge",
    )(x, oa, ob, *ocs, *lss, y, y, y, wb, wo)


def _rms(x, gain):
    ms = jnp.mean(x * x, axis=-1, keepdims=True)
    return x * lax.rsqrt(ms + EPS) * gain


def _finish(x_ref, acc_ref, o_ref, fg_ref):
    out = x_ref[...] + acc_ref[...]
    if fg_ref is not None:
        out = _rms(out, fg_ref[...])
    o_ref[...] = out


def _ffn_kernel(x_ref, g_ref, w1_ref, w3_ref, w2_ref, *rest, nf, final_norm):
    if final_norm:
        fg_ref, o_ref, h_ref, acc_ref = rest
    else:
        fg_ref = None
        o_ref, h_ref, acc_ref = rest
    j = pl.program_id(1)

    @pl.when(j == 0)
    def _():
        h_ref[...] = _rms(x_ref[...], g_ref[...]).astype(BF16)
        acc_ref[...] = jnp.zeros(acc_ref.shape, F32)

    h = h_ref[...]
    a = jnp.dot(h, w1_ref[...], preferred_element_type=F32)
    b = jnp.dot(h, w3_ref[...], preferred_element_type=F32)
    t = (jax.nn.silu(a) * b).astype(BF16)
    acc_ref[...] += jnp.dot(t, w2_ref[...], preferred_element_type=F32)

    @pl.when(j == nf - 1)
    def _():
        _finish(x_ref, acc_ref, o_ref, fg_ref)


def _ffn(x, gain, w1, w3, w2, final_gain=None, *, tm=512, tf=1408):
    n, d = x.shape
    ff = w1.shape[1]
    assert n % tm == 0 and ff % tf == 0
    nf = ff // tf
    final_norm = final_gain is not None
    in_specs = [
        pl.BlockSpec((tm, d), lambda i, j: (i, 0)),
        pl.BlockSpec((1, d), lambda i, j: (0, 0)),
        pl.BlockSpec((d, tf), lambda i, j: (0, j)),
        pl.BlockSpec((d, tf), lambda i, j: (0, j)),
        pl.BlockSpec((tf, d), lambda i, j: (j, 0)),
    ]
    args = [x, gain.reshape(1, d), w1, w3, w2]
    if final_norm:
        in_specs.append(pl.BlockSpec((1, d), lambda i, j: (0, 0)))
        args.append(final_gain.reshape(1, d))
    return pl.pallas_call(
        functools.partial(_ffn_kernel, nf=nf, final_norm=final_norm),
        grid=(n // tm, nf),
        in_specs=in_specs,
        out_specs=pl.BlockSpec((tm, d), lambda i, j: (i, 0)),
        out_shape=jax.ShapeDtypeStruct((n, d), F32),
        scratch_shapes=[pltpu.VMEM((tm, d), BF16), pltpu.VMEM((tm, d), F32)],
        compiler_params=_cparams(("parallel", "arbitrary")),
        name="ffn",
    )(*args)


def _top2_gate(logits):
    lane = lax.broadcasted_iota(jnp.int32, logits.shape, 1)
    m1 = jnp.max(logits, axis=-1, keepdims=True)
    i1 = jnp.min(jnp.where(logits == m1, lane, LANES), axis=-1, keepdims=True)
    rest = jnp.where(lane == i1, -jnp.inf, logits)
    m2 = jnp.max(rest, axis=-1, keepdims=True)
    i2 = jnp.min(jnp.where(rest == m2, lane, LANES), axis=-1, keepdims=True)
    e2 = jnp.exp(m2 - m1)
    w1 = 1.0 / (1.0 + e2)
    w2 = e2 / (1.0 + e2)
    return jnp.where(lane == i1, w1, 0.0) + jnp.where(lane == i2, w2, 0.0)


def _moe_dense_kernel(x_ref, g_ref, r_ref, w1_ref, w3_ref, w2_ref, *rest, ne, nf, final_norm):
    if final_norm:
        fg_ref, o_ref, h_ref, acc_ref, gate_ref = rest
    else:
        fg_ref = None
        o_ref, h_ref, acc_ref, gate_ref = rest
    e = pl.program_id(1)
    j = pl.program_id(2)

    @pl.when((e == 0) & (j == 0))
    def _():
        hf = _rms(x_ref[...], g_ref[...])
        h_ref[...] = hf.astype(BF16)
        acc_ref[...] = jnp.zeros(acc_ref.shape, F32)
        logits = jnp.dot(hf, r_ref[...], preferred_element_type=F32, precision=lax.Precision.HIGHEST)
        lane = lax.broadcasted_iota(jnp.int32, logits.shape, 1)
        gate_ref[...] = _top2_gate(jnp.where(lane < N_EXPERTS, logits, -jnp.inf))

    lane = lax.broadcasted_iota(jnp.int32, gate_ref.shape, 1)
    ge = jnp.sum(jnp.where(lane == e, gate_ref[...], 0.0), axis=-1, keepdims=True)
    h = h_ref[...]
    a = jnp.dot(h, w1_ref[0], preferred_element_type=F32)
    b = jnp.dot(h, w3_ref[0], preferred_element_type=F32)
    t = (jax.nn.silu(a) * b * ge).astype(BF16)
    acc_ref[...] += jnp.dot(t, w2_ref[0], preferred_element_type=F32)

    @pl.when((e == ne - 1) & (j == nf - 1))
    def _():
        _finish(x_ref, acc_ref, o_ref, fg_ref)


def _moe_dense(x, gain, router_p, w1, w3, w2, final_gain=None, *, tm=512, tf=1408):
    n, d = x.shape
    ne, _, ff = w1.shape
    assert n % tm == 0 and ff % tf == 0
    nf = ff // tf
    final_norm = final_gain is not None
    in_specs = [
        pl.BlockSpec((tm, d), lambda i, e, j: (i, 0)),
        pl.BlockSpec((1, d), lambda i, e, j: (0, 0)),
        pl.BlockSpec((d, LANES), lambda i, e, j: (0, 0)),
        pl.BlockSpec((1, d, tf), lambda i, e, j: (e, 0, j)),
        pl.BlockSpec((1, d, tf), lambda i, e, j: (e, 0, j)),
        pl.BlockSpec((1, tf, d), lambda i, e, j: (e, j, 0)),
    ]
    args = [x, gain.reshape(1, d), router_p, w1, w3, w2]
    if final_norm:
        in_specs.append(pl.BlockSpec((1, d), lambda i, e, j: (0, 0)))
        args.append(final_gain.reshape(1, d))
    return pl.pallas_call(
        functools.partial(_moe_dense_kernel, ne=ne, nf=nf, final_norm=final_norm),
        grid=(n // tm, ne, nf),
        in_specs=in_specs,
        out_specs=pl.BlockSpec((tm, d), lambda i, e, j: (i, 0)),
        out_shape=jax.ShapeDtypeStruct((n, d), F32),
        scratch_shapes=[pltpu.VMEM((tm, d), BF16), pltpu.VMEM((tm, d), F32),
                        pltpu.VMEM((tm, LANES), F32)],
        compiler_params=_cparams(("parallel", "arbitrary", "arbitrary")),
        name="moe",
    )(*args)


def _t5_bucket(rel):
    nb = NUM_BUCKETS // 2
    max_exact = nb // 2
    ret = jnp.where(rel > 0, nb, 0)
    n = jnp.abs(rel)
    large = max_exact + (jnp.log(jnp.maximum(n, max_exact).astype(F32) / max_exact)
                         / math.log(MAX_DISTANCE / max_exact) * (nb - max_exact)).astype(jnp.int32)
    large = jnp.minimum(large, nb - 1)
    return ret + jnp.where(n < max_exact, n, large)


def _rope_tables(seq):
    t = jnp.arange(seq)
    row = (t // GRID_W).astype(F32)
    col = (t % GRID_W).astype(F32)
    half = HEAD_DIM // 2
    inv = ROPE_THETA ** (-jnp.arange(0, half, 2, dtype=F32) / half)
    ar = row[:, None] * inv
    ac = col[:, None] * inv
    cos = jnp.concatenate([jnp.cos(ar), jnp.cos(ar), jnp.cos(ac), jnp.cos(ac)], axis=1)
    sin = jnp.concatenate([-jnp.sin(ar), jnp.sin(ar), -jnp.sin(ac), jnp.sin(ac)], axis=1)
    return jnp.tile(cos, (1, LANES // HEAD_DIM)), jnp.tile(sin, (1, LANES // HEAD_DIM))


def _bias_tiles_b(rel_bias, t):
    cols = rel_bias[:, :B_HEADS].astype(F32) * LOG2E
    i = jnp.arange(t)[:, None]
    j = jnp.arange(t)[None, :]
    tiles = []
    for delta in (-2, -1, 0, 1, 2):
        if abs(delta) == 2:
            rel = jnp.full((t, t), delta * t, jnp.int32)
        else:
            rel = delta * t + j - i
        tiles.append(cols[_t5_bucket(rel)])
    return jnp.stack(tiles, axis=0).transpose(3, 0, 1, 2)


def _bias_tile_c(rel_bias, g, dil, span, tq, halo):
    cols = rel_bias[:, B_HEADS + g * C_HEADS:B_HEADS + (g + 1) * C_HEADS].astype(F32) * LOG2E
    i = jnp.arange(tq)[:, None]
    j = jnp.arange(tq + 2 * halo)[None, :]
    rel = j - halo - i
    b = cols[_t5_bucket(rel * dil)].transpose(2, 0, 1)
    return jnp.where((jnp.abs(rel) <= span)[None], b, NEG_BIG)


def _permute_cols(w):
    split = IN_COLS - GATE_COLS
    return jnp.concatenate([w[..., split:], w[..., :split]], axis=-1)


def _trunk(x3, p):
    bsz, seq, d = x3.shape
    n = bsz * seq
    x = x3.reshape(n, d)
    depth = p["ln1"].shape[0]
    assert seq >= MAX_DISTANCE
    cos_t, sin_t = _rope_tables(seq)
    tb = min(512, seq)
    bias_b = _bias_tiles_b(p["rel_bias"], tb)
    win_tq, win_halo = 128, 128
    bias_c = []
    for g, (window, dil) in enumerate(C_GROUPS):
        span = window // (2 * dil)
        assert span <= win_halo
        bias_c.append(_bias_tile_c(p["rel_bias"], g, dil, span, win_tq, win_halo))

    for l in range(depth):
        y = _norm_proj(x, p["ln1"][l], p["w_in"][l], p["colscale"])
        qn, kd, vd = _prep_a(y, cos_t, sin_t, p["qg"][l], p["kg"][l], p["bd"], seq)
        oa = _flash_a(qn, kd, vd, bsz, seq)
        lambda_init = 0.8 - 0.6 * math.exp(-0.3 * l)
        ob = _flash_b(y, bias_b, p["lam"][l], p["subln"][l], bsz, seq, lambda_init, t=tb)
        ocs, lss = [], []
        for g, (window, dil) in enumerate(C_GROUPS):
            y3 = y.reshape(bsz, seq // dil, dil * IN_COLS)
            oc, ls = _window_group(y3, bias_c[g], g, dil, bsz, seq, tq=win_tq, halo=win_halo)
            ocs.append(oc.reshape(n, C_OUT))
            lss.append(ls.reshape(n, C_OUT))
        x = _merge(x, oa, ob, ocs, lss, y, p["w_br"][l], p["w_o"][l])
        fg = p["ln_f"] if l == depth - 1 else None
        jj = l // 2
        if l % 2 == 0:
            x = _ffn(x, p["ln2"][l], p["ffn_w1"][jj], p["ffn_w3"][jj], p["ffn_w2"][jj], fg)
        else:
            x = _moe_dense(x, p["ln2"][l], p["router"][jj], p["exp_w1"][jj], p["exp_w3"][jj],
                           p["exp_w2"][jj], fg)
    if depth == 0:
        raise ValueError("depth must be positive")
    return x.reshape(bsz, seq, d)


def kernel(x_prompt, x_sample, rel_bias, ln1, ln2, ln_f, w_in, a_qnorm, a_knorm, lam_q1, lam_k1,
           lam_q2, lam_k2, b_subln, w_br, w_o, ffn_w1, ffn_w3, ffn_w2, router, exp_w1, exp_w3, exp_w2):
    depth = ln1.shape[0]
    colscale = jnp.ones((1, IN_COLS), F32)
    colscale = colscale.at[:, OFF_BQ:OFF_BQ + B_QK].set(QK_SCALE)
    colscale = colscale.at[:, OFF_CQ:OFF_CQ + C_QKV].set(QK_SCALE)
    lam = (jnp.exp(jnp.sum(lam_q1.astype(F32) * lam_k1.astype(F32), axis=-1))
           - jnp.exp(jnp.sum(lam_q2.astype(F32) * lam_k2.astype(F32), axis=-1)))
    lam = lam + jnp.asarray([0.8 - 0.6 * math.exp(-0.3 * l) for l in range(depth)], F32)
    head_id = np.arange(LANES) // HEAD_DIM
    p = {
        "rel_bias": rel_bias,
        "ln1": ln1, "ln2": ln2, "ln_f": ln_f,
        "w_in": _permute_cols(w_in).astype(BF16),
        "colscale": colscale,
        "qg": jnp.tile(a_qnorm.astype(F32), (1, LANES // HEAD_DIM)).reshape(depth, 1, LANES),
        "kg": jnp.tile(a_knorm.astype(F32), (1, LANES // HEAD_DIM)).reshape(depth, 1, LANES),
        "bd": jnp.asarray(head_id[:, None] == head_id[None, :], F32),
        "lam": lam.reshape(depth, 1),
        "subln": b_subln.astype(F32).reshape(depth, 1, 2 * HEAD_DIM),
        "w_br": w_br.astype(BF16), "w_o": w_o.astype(BF16),
        "ffn_w1": ffn_w1.astype(BF16), "ffn_w3": ffn_w3.astype(BF16), "ffn_w2": ffn_w2.astype(BF16),
        "router": jnp.pad(router.astype(F32), ((0, 0), (0, 0), (0, LANES - N_EXPERTS))),
        "exp_w1": exp_w1.astype(BF16), "exp_w3": exp_w3.astype(BF16), "exp_w2": exp_w2.astype(BF16),
    }
    return (_trunk(x_prompt, p), _trunk(x_sample, p))
```

```python
import functools
import math

import jax
import jax.numpy as jnp
import numpy as np
from jax import lax
from jax.experimental import pallas as pl
from jax.experimental.pallas import tpu as pltpu

F32 = jnp.float32
BF16 = jnp.bfloat16

D_MODEL = 1024
HEAD_DIM = 64
GRID_W = 64
ROPE_THETA = 10000.0
EPS = 1e-6
A_HEADS = 8
A_KV_HEADS = 2
B_HEADS = 4
C_GROUPS = ((128, 1), (512, 4), (2048, 16))
C_HEADS = 4
NUM_BUCKETS = 32
MAX_DISTANCE = 128
N_EXPERTS = 8
TOP_K = 2
N_BRANCH = 3

A_Q = A_HEADS * HEAD_DIM
A_KV = A_KV_HEADS * HEAD_DIM
B_QK = B_HEADS * 2 * HEAD_DIM
B_V = B_HEADS * 2 * HEAD_DIM
N_CGROUPS = len(C_GROUPS)
C_QKV = N_CGROUPS * C_HEADS * HEAD_DIM
C_OUT = C_HEADS * HEAD_DIM
GATE_COLS = N_BRANCH * D_MODEL
IN_COLS = A_Q + 2 * A_KV + 2 * B_QK + B_V + 3 * C_QKV + GATE_COLS

OFF_GATE = 0
OFF_AQ = GATE_COLS
A_SLAB = A_Q + 2 * A_KV
OFF_BQ = OFF_AQ + A_SLAB
OFF_BK = OFF_BQ + B_QK
OFF_BV = OFF_BK + B_QK
MAIN_COLS = OFF_BV + B_V
C_SLAB = 3 * C_OUT

LANES = 128
SUBLANES = 8
VMEM_LIMIT = 56 * 1024 * 1024
LOG2E = 1.4426950408889634
LN2 = 0.6931471805599453
QK_SCALE = HEAD_DIM ** -0.5 * LOG2E
NEG_BIG = -1e30

NT_DIMS = (((1,), (1,)), ((), ()))
TN_DIMS = (((0,), (0,)), ((), ()))


def _cparams(sem):
    return pltpu.CompilerParams(dimension_semantics=sem, vmem_limit_bytes=VMEM_LIMIT)


def _norm_proj_kernel(x_ref, g_ref, w_ref, cs_ref, o_ref, c0_ref, c1_ref, c2_ref, h_ref, ys_ref,
                      *, n_gate_tiles, n_main_tiles, dils):
    j = pl.program_id(1)
    tm = x_ref.shape[0]

    @pl.when(j == 0)
    def _():
        x = x_ref[...]
        ms = jnp.mean(x * x, axis=-1, keepdims=True)
        h_ref[...] = (x * lax.rsqrt(ms + EPS) * g_ref[...]).astype(BF16)

    y = jnp.dot(h_ref[...], w_ref[...], preferred_element_type=F32) * cs_ref[...]

    @pl.when(j < n_gate_tiles)
    def _():
        o_ref[...] = jax.nn.sigmoid(y).astype(BF16)

    @pl.when((j >= n_gate_tiles) & (j < n_main_tiles))
    def _():
        o_ref[...] = y.astype(BF16)

    for g, (c_ref, dil) in enumerate(zip((c0_ref, c1_ref, c2_ref), dils)):
        @pl.when(j == n_main_tiles + g)
        def _(c_ref=c_ref, dil=dil):
            if dil == 1:
                c_ref[0, 0] = y.astype(BF16)
            else:
                for c in range(y.shape[1] // LANES):
                    ys_ref[c] = y[:, c * LANES:(c + 1) * LANES]
                for r in range(dil):
                    for c in range(y.shape[1] // LANES):
                        c_ref[0, r, :, c * LANES:(c + 1) * LANES] = ys_ref[
                            c, pl.ds(r, tm // dil, stride=dil), :].astype(BF16)


def _norm_proj(x, gain, w, colscale, bsz, seq, *, tm=1024, tn=C_SLAB):
    n, d = x.shape
    tm = min(tm, seq)
    dils = tuple(dil for _, dil in C_GROUPS)
    assert seq % tm == 0 and tn == C_SLAB and MAIN_COLS % tn == 0 and GATE_COLS % tn == 0
    assert all(tm % (2 * SUBLANES * dil) == 0 for dil in dils) and len(dils) == 3
    n_main = MAIN_COLS // tn
    tps = seq // tm

    def c_spec(dil):
        return pl.BlockSpec((1, dil, tm // dil, tn), lambda i, j: (i // tps, 0, i % tps, 0))

    return pl.pallas_call(
        functools.partial(_norm_proj_kernel, n_gate_tiles=GATE_COLS // tn, n_main_tiles=n_main, dils=dils),
        grid=(n // tm, n_main + len(dils)),
        in_specs=[
            pl.BlockSpec((tm, d), lambda i, j: (i, 0)),
            pl.BlockSpec((1, d), lambda i, j: (0, 0)),
            pl.BlockSpec((d, tn), lambda i, j: (0, j)),
            pl.BlockSpec((1, tn), lambda i, j: (0, j)),
        ],
        out_specs=[pl.BlockSpec((tm, tn), lambda i, j: (i, jnp.minimum(j, n_main - 1)))]
        + [c_spec(dil) for dil in dils],
        out_shape=[jax.ShapeDtypeStruct((n, MAIN_COLS), BF16)]
        + [jax.ShapeDtypeStruct((bsz, dil, seq // dil, tn), BF16) for dil in dils],
        scratch_shapes=[pltpu.VMEM((tm, d), BF16), pltpu.VMEM((tn // LANES, tm, LANES), F32)],
        compiler_params=_cparams(("parallel", "arbitrary")),
        name="norm_proj",
    )(x, gain.reshape(1, d), w, colscale)


def _prep_a_kernel(y_ref, cos_ref, sin_ref, qg_ref, kg_ref, bd_ref, q_out, k_out, v_out):
    tm = y_ref.shape[0]
    lane = lax.broadcasted_iota(jnp.int32, (tm, LANES), 1)
    first16 = (lane % 32) < 16
    half0 = lane < HEAD_DIM
    cos = cos_ref[...]
    sin = sin_ref[...]
    bd = bd_ref[...]

    def norm_rope(xb, gain, out_scale):
        x = xb.astype(F32)
        ss = jnp.dot(x * x, bd, preferred_element_type=F32, precision=lax.Precision.HIGHEST)
        xn = x * lax.rsqrt(ss * (1.0 / HEAD_DIM) + EPS) * gain
        sw = jnp.where(first16, pltpu.roll(xn, LANES - 16, 1), pltpu.roll(xn, 16, 1))
        return (xn * cos + sw * sin) * out_scale

    qg = qg_ref[...]
    for c in range(A_Q // LANES):
        q_out[:, c * LANES:(c + 1) * LANES] = norm_rope(
            y_ref[:, c * LANES:(c + 1) * LANES], qg, QK_SCALE).astype(BF16)

    k = norm_rope(y_ref[:, A_Q:A_Q + LANES], kg_ref[...], 1.0)
    kr = pltpu.roll(k, HEAD_DIM, 1)
    k_out[:, 0:LANES] = jnp.where(half0, k, kr).astype(BF16)
    k_out[:, LANES:2 * LANES] = jnp.where(half0, kr, k).astype(BF16)

    v = y_ref[:, A_Q + LANES:A_Q + 2 * LANES].astype(F32)
    vr = pltpu.roll(v, HEAD_DIM, 1)
    v_out[:, 0:LANES] = jnp.where(half0, v, vr).astype(BF16)
    v_out[:, LANES:2 * LANES] = jnp.where(half0, vr, v).astype(BF16)


def _prep_a(y, cos_t, sin_t, qg, kg, bd, seq, *, tm=512):
    n = y.shape[0]
    assert OFF_AQ % A_SLAB == 0 and seq % tm == 0 and A_KV == LANES
    spt = seq // tm
    return pl.pallas_call(
        _prep_a_kernel,
        grid=(n // tm,),
        in_specs=[
            pl.BlockSpec((tm, A_SLAB), lambda i: (i, OFF_AQ // A_SLAB)),
            pl.BlockSpec((tm, LANES), lambda i: (i % spt, 0)),
            pl.BlockSpec((tm, LANES), lambda i: (i % spt, 0)),
            pl.BlockSpec((1, LANES), lambda i: (0, 0)),
            pl.BlockSpec((1, LANES), lambda i: (0, 0)),
            pl.BlockSpec((LANES, LANES), lambda i: (0, 0)),
        ],
        out_specs=[
            pl.BlockSpec((tm, A_Q), lambda i: (i, 0)),
            pl.BlockSpec((tm, 2 * LANES), lambda i: (i, 0)),
            pl.BlockSpec((tm, 2 * LANES), lambda i: (i, 0)),
        ],
        out_shape=[
            jax.ShapeDtypeStruct((n, A_Q), BF16),
            jax.ShapeDtypeStruct((n, 2 * LANES), BF16),
            jax.ShapeDtypeStruct((n, 2 * LANES), BF16),
        ],
        compiler_params=_cparams(("parallel",)),
        name="prep_a",
    )(y, cos_t, sin_t, qg, kg, bd)


def _colmax(s):
    tk, r = s.shape
    m8 = jnp.max(s.reshape(tk // SUBLANES, SUBLANES, r), axis=0)
    return jnp.max(m8, axis=0, keepdims=True)


def _softmax_pv(s, mb, v_ref, m_ref, l_ref, acc_ref):
    tk, r = s.shape
    m_prev = m_ref[...]
    m_cur = jnp.maximum(m_prev, mb)
    alpha = jnp.exp2(m_prev - m_cur)
    p = jnp.exp2(s - m_cur)
    l_ref[...] = alpha * l_ref[...] + jnp.sum(p.reshape(tk // SUBLANES, SUBLANES, r), axis=0)
    acc_ref[...] = alpha * acc_ref[...] + lax.dot_general(
        v_ref[...], p.astype(BF16), TN_DIMS, preferred_element_type=F32)
    m_ref[...] = m_cur


def _flash_steps(t, nk, scores_fn, init_fn, finish_fn, v_ref, s_ref, mb_ref, m_ref, l_ref, acc_ref):
    def produce(cur):
        s_new = scores_fn()
        mb_ref[cur] = _colmax(s_new)
        s_ref[cur] = s_new

    def consume(prev):
        _softmax_pv(s_ref[prev], mb_ref[prev], v_ref, m_ref, l_ref, acc_ref)

    @pl.when(t == 0)
    def _():
        m_ref[...] = jnp.full(m_ref.shape, NEG_BIG, F32)
        l_ref[...] = jnp.zeros(l_ref.shape, F32)
        acc_ref[...] = jnp.zeros(acc_ref.shape, F32)
        init_fn()
        produce(0)

    for par in (0, 1):
        @pl.when((t > 0) & (t < nk) & (t % 2 == par))
        def _(par=par):
            produce(par)
            consume(1 - par)

    @pl.when(t == nk)
    def _():
        consume((nk - 1) % 2)
        finish_fn()


def _flash_scratch(tk, r):
    return [
        pltpu.VMEM((r, LANES), BF16),
        pltpu.VMEM((2, tk, r), F32),
        pltpu.VMEM((2, 1, r), F32),
        pltpu.VMEM((1, r), F32),
        pltpu.VMEM((SUBLANES, r), F32),
        pltpu.VMEM((LANES, r), F32),
    ]


def _flash_a_kernel(q_ref, k_ref, v_ref, o_ref, qs_ref, s_ref, mb_ref, m_ref, l_ref, acc_ref, *, tq, nk):
    t = pl.program_id(3)
    grp = A_HEADS // A_KV_HEADS
    lane = lax.broadcasted_iota(jnp.int32, (tq, LANES), 1)
    half0 = lane < HEAD_DIM

    def init():
        for h in range(grp):
            c = h // 2
            x = q_ref[:, c * LANES:(c + 1) * LANES]
            keep = half0 if h % 2 == 0 else jnp.logical_not(half0)
            qs_ref[h * tq:(h + 1) * tq, :] = jnp.where(keep, x, jnp.zeros_like(x))

    def scores():
        return lax.dot_general(k_ref[...], qs_ref[...], NT_DIMS, preferred_element_type=F32)

    def finish():
        o_t = acc_ref[...] / jnp.sum(l_ref[...], axis=0, keepdims=True)
        for c in range(grp // 2):
            o0 = o_t[:, (2 * c) * tq:(2 * c + 1) * tq].T
            o1 = o_t[:, (2 * c + 1) * tq:(2 * c + 2) * tq].T
            o_ref[:, c * LANES:(c + 1) * LANES] = jnp.where(half0, o0, o1).astype(BF16)

    _flash_steps(t, nk, scores, init, finish, v_ref, s_ref, mb_ref, m_ref, l_ref, acc_ref)


def _flash_a(qn, kd, vd, bsz, seq, *, tq=512, tk=1024):
    n = qn.shape[0]
    tk = min(tk, seq)
    assert seq % tq == 0 and seq % tk == 0
    nq, nk = seq // tq, seq // tk
    grp = A_HEADS // A_KV_HEADS
    qw = grp * HEAD_DIM
    return pl.pallas_call(
        functools.partial(_flash_a_kernel, tq=tq, nk=nk),
        grid=(bsz, A_KV_HEADS, nq, nk + 1),
        in_specs=[
            pl.BlockSpec((tq, qw), lambda b, kv, qi, t: (b * nq + qi, kv)),
            pl.BlockSpec((tk, LANES), lambda b, kv, qi, t: (b * nk + jnp.minimum(t, nk - 1), kv)),
            pl.BlockSpec((tk, LANES), lambda b, kv, qi, t: (b * nk + jnp.maximum(t - 1, 0), kv)),
        ],
        out_specs=pl.BlockSpec((tq, qw), lambda b, kv, qi, t: (b * nq + qi, kv)),
        out_shape=jax.ShapeDtypeStruct((n, A_Q), BF16),
        scratch_shapes=_flash_scratch(tk, grp * tq),
        compiler_params=_cparams(("parallel", "parallel", "parallel", "arbitrary")),
        name="flash_a",
    )(qn, kd, vd)


def _flash_b_kernel(lam_ref, q_ref, k_ref, v_ref, bias_ref, sub_ref, o_ref,
                    qs_ref, s_ref, mb_ref, m_ref, l_ref, acc_ref, *, tq, nk, out_scale):
    qi = pl.program_id(2)
    t = pl.program_id(3)
    lane = lax.broadcasted_iota(jnp.int32, (tq, LANES), 1)
    half0 = lane < HEAD_DIM

    def init():
        x = q_ref[...]
        zero = jnp.zeros_like(x)
        qs_ref[0:tq, :] = jnp.where(half0, x, zero)
        qs_ref[tq:2 * tq, :] = jnp.where(half0, zero, x)

    def scores():
        bias = bias_ref[0, jnp.clip(t - qi, -2, 2) + 2]
        k = k_ref[...]
        s0 = lax.dot_general(k, qs_ref[0:tq, :], NT_DIMS, preferred_element_type=F32) + bias
        s1 = lax.dot_general(k, qs_ref[tq:2 * tq, :], NT_DIMS, preferred_element_type=F32) + bias
        return jnp.concatenate([s0, s1], axis=1)

    def finish():
        lam = lam_ref[0]
        o_t = acc_ref[...] / jnp.sum(l_ref[...], axis=0, keepdims=True)
        o = (o_t[:, 0:tq] - lam * o_t[:, tq:2 * tq]).T
        ms = jnp.mean(o * o, axis=-1, keepdims=True)
        o_ref[...] = (o * lax.rsqrt(ms + EPS) * sub_ref[...] * out_scale).astype(BF16)

    _flash_steps(t, nk, scores, init, finish, v_ref, s_ref, mb_ref, m_ref, l_ref, acc_ref)


def _flash_b(y, bias_tiles, lam, subln, bsz, seq, lambda_init, *, t):
    n = y.shape[0]
    assert seq % t == 0 and bias_tiles.shape == (B_HEADS, 5, t, t)
    nq = seq // t
    qb, kb, vb = OFF_BQ // LANES, OFF_BK // LANES, OFF_BV // LANES
    return pl.pallas_call(
        functools.partial(_flash_b_kernel, tq=t, nk=nq, out_scale=1.0 - lambda_init),
        grid=(B_HEADS, bsz, nq, nq + 1),
        in_specs=[
            pl.BlockSpec(memory_space=pltpu.SMEM),
            pl.BlockSpec((t, LANES), lambda h, b, qi, ts: (b * nq + qi, qb + h)),
            pl.BlockSpec((t, LANES), lambda h, b, qi, ts: (b * nq + jnp.minimum(ts, nq - 1), kb + h)),
            pl.BlockSpec((t, LANES), lambda h, b, qi, ts: (b * nq + jnp.maximum(ts - 1, 0), vb + h)),
            pl.BlockSpec((1, 5, t, t), lambda h, b, qi, ts: (h, 0, 0, 0)),
            pl.BlockSpec((1, LANES), lambda h, b, qi, ts: (0, 0)),
        ],
        out_specs=pl.BlockSpec((t, LANES), lambda h, b, qi, ts: (b * nq + qi, h)),
        out_shape=jax.ShapeDtypeStruct((n, B_V), BF16),
        scratch_shapes=_flash_scratch(t, 2 * t),
        compiler_params=_cparams(("parallel", "parallel", "parallel", "arbitrary")),
        name="flash_b",
    )(lam, y, y, y, bias_tiles, subln)


def _window_kernel(q_ref, kp_ref, km_ref, kn_ref, vp_ref, vm_ref, vn_ref, bias_ref,
                   o_ref, lse_ref, *, tq, halo, length, dil):
    i = pl.program_id(1)
    nkeys = tq + 2 * halo
    ukey = i * tq - halo + lax.broadcasted_iota(jnp.int32, (1, nkeys), 1)
    colmask = jnp.where((ukey >= 0) & (ukey < length), 0.0, NEG_BIG).astype(F32)
    lane = lax.broadcasted_iota(jnp.int32, (tq, LANES), 1)
    half0 = lane < HEAD_DIM

    def one_residue(r, rows):
        q = q_ref[0, r]
        kcat = jnp.concatenate([kp_ref[0, r], km_ref[0, r], kn_ref[0, r]], axis=0)
        vcat = jnp.concatenate([vp_ref[0, r], vm_ref[0, r], vn_ref[0, r]], axis=0)
        for c in range(C_HEADS // 2):
            qc = q[:, c * LANES:(c + 1) * LANES]
            kc = kcat[:, c * LANES:(c + 1) * LANES]
            vc = vcat[:, c * LANES:(c + 1) * LANES]
            outs, lses = [], []
            for hh in range(2):
                keep = half0 if hh == 0 else jnp.logical_not(half0)
                qh = jnp.where(keep, qc, jnp.zeros_like(qc))
                s = lax.dot_general(qh, kc, NT_DIMS, preferred_element_type=F32)
                s = s + bias_ref[2 * c + hh] + colmask
                m = jnp.max(s, axis=-1, keepdims=True)
                e = jnp.exp2(s - m)
                den = jnp.sum(e, axis=-1, keepdims=True)
                outs.append(jnp.dot(e.astype(BF16), vc, preferred_element_type=F32) / den)
                lses.append(jnp.broadcast_to((m + jnp.log2(den)) * LN2, (tq, LANES)))
            o_ref[c, rows, :] = jnp.where(half0, outs[0], outs[1])
            lse_ref[c, rows, :] = jnp.where(half0, lses[0], lses[1])

    if dil == 1:
        one_residue(0, pl.ds(0, tq))
    else:
        def body(r, carry):
            one_residue(r, pl.ds(r, tq, stride=dil))
            return carry
        lax.fori_loop(0, dil, body, 0)


def _window_group(cg, bias_tile, dil, bsz, seq, *, tq, halo):
    length = seq // dil
    tq = min(tq, length)
    cw = C_OUT
    assert length % tq == 0 and tq % halo == 0 and length % halo == 0
    assert bias_tile.shape == (C_HEADS, tq, tq + 2 * halo)
    nt = length // tq
    hb = tq // halo
    nhb = length // halo
    main = lambda col: pl.BlockSpec((1, dil, tq, cw), lambda b, i: (b, 0, i, col))
    prev = lambda col: pl.BlockSpec(
        (1, dil, halo, cw), lambda b, i: (b, 0, jnp.maximum(i * hb - 1, 0), col))
    nxt = lambda col: pl.BlockSpec(
        (1, dil, halo, cw), lambda b, i: (b, 0, jnp.minimum((i + 1) * hb, nhb - 1), col))
    out = pl.BlockSpec((cw // LANES, tq * dil, LANES), lambda b, i: (0, b * nt + i, 0))
    return pl.pallas_call(
        functools.partial(_window_kernel, tq=tq, halo=halo, length=length, dil=dil),
        grid=(bsz, nt),
        in_specs=[main(0), prev(1), main(1), nxt(1), prev(2), main(2), nxt(2),
                  pl.BlockSpec(bias_tile.shape, lambda b, i: (0, 0, 0))],
        out_specs=[out, out],
        out_shape=[jax.ShapeDtypeStruct((cw // LANES, bsz * seq, LANES), F32)] * 2,
        compiler_params=_cparams(("parallel", "parallel")),
        name=f"window_d{dil}",
    )(cg, cg, cg, cg, cg, cg, cg, bias_tile)


def _merge_kernel(x_ref, oa_ref, ob_ref, oc0_ref, oc1_ref, oc2_ref, ls0_ref, ls1_ref, ls2_ref,
                  g0_ref, g1_ref, g2_ref, wb_ref, wo_ref, o_ref):
    planes = lambda ref: jnp.concatenate([ref[c] for c in range(C_OUT // LANES)], axis=1)
    ls = [planes(ls0_ref), planes(ls1_ref), planes(ls2_ref)]
    ocs = [planes(oc0_ref), planes(oc1_ref), planes(oc2_ref)]
    m = jnp.maximum(jnp.maximum(ls[0], ls[1]), ls[2])
    es = [jnp.exp(l - m) for l in ls]
    den = es[0] + es[1] + es[2]
    oc = (es[0] * ocs[0] + es[1] * ocs[1] + es[2] * ocs[2]) / den
    ya = jnp.dot(oa_ref[...], wb_ref[0:A_Q, :], preferred_element_type=F32)
    yb = jnp.dot(ob_ref[...], wb_ref[A_Q:A_Q + B_V, :], preferred_element_type=F32)
    yc = jnp.dot(oc.astype(BF16), wb_ref[A_Q + B_V:, :], preferred_element_type=F32)
    merged = (g0_ref[...].astype(F32) * ya + g1_ref[...].astype(F32) * yb
              + g2_ref[...].astype(F32) * yc)
    o_ref[...] = x_ref[...] + jnp.dot(merged.astype(BF16), wo_ref[...], preferred_element_type=F32)


def _merge(x, oa, ob, ocs, lss, y, wb, wo, *, tm=512):
    n, d = x.shape
    assert n % tm == 0 and OFF_GATE == 0
    row = lambda w: pl.BlockSpec((tm, w), lambda i: (i, 0))
    gate = lambda j: pl.BlockSpec((tm, d), lambda i: (i, j))
    plane = pl.BlockSpec((C_OUT // LANES, tm, LANES), lambda i: (0, i, 0))
    return pl.pallas_call(
        _merge_kernel,
        grid=(n // tm,),
        in_specs=[row(d), row(A_Q), row(B_V)] + [plane] * 6 + [gate(0), gate(1), gate(2)] + [
            pl.BlockSpec(wb.shape, lambda i: (0, 0)),
            pl.BlockSpec(wo.shape, lambda i: (0, 0)),
        ],
        out_specs=row(d),
        out_shape=jax.ShapeDtypeStruct((n, d), F32),
        compiler_params=_cparams(("parallel",)),
        name="merge",
    )(x, oa, ob, *ocs, *lss, y, y, y, wb, wo)


def _rms(x, gain):
    ms = jnp.mean(x * x, axis=-1, keepdims=True)
    return x * lax.rsqrt(ms + EPS) * gain


def _finish(x_ref, acc_ref, o_ref, fg_ref):
    out = x_ref[...] + acc_ref[...]
    if fg_ref is not None:
        out = _rms(out, fg_ref[...])
    o_ref[...] = out


def _ffn_kernel(x_ref, g_ref, w1_ref, w3_ref, w2_ref, *rest, nf, final_norm):
    if final_norm:
        fg_ref, o_ref, h_ref, acc_ref = rest
    else:
        fg_ref = None
        o_ref, h_ref, acc_ref = rest
    j = pl.program_id(1)

    @pl.when(j == 0)
    def _():
        h_ref[...] = _rms(x_ref[...], g_ref[...]).astype(BF16)
        acc_ref[...] = jnp.zeros(acc_ref.shape, F32)

    h = h_ref[...]
    a = jnp.dot(h, w1_ref[...], preferred_element_type=F32)
    b = jnp.dot(h, w3_ref[...], preferred_element_type=F32)
    t = (jax.nn.silu(a) * b).astype(BF16)
    acc_ref[...] += jnp.dot(t, w2_ref[...], preferred_element_type=F32)

    @pl.when(j == nf - 1)
    def _():
        _finish(x_ref, acc_ref, o_ref, fg_ref)


def _ffn(x, gain, w1, w3, w2, final_gain=None, *, tm=512, tf=1408):
    n, d = x.shape
    ff = w1.shape[1]
    assert n % tm == 0 and ff % tf == 0
    nf = ff // tf
    final_norm = final_gain is not None
    in_specs = [
        pl.BlockSpec((tm, d), lambda i, j: (i, 0)),
        pl.BlockSpec((1, d), lambda i, j: (0, 0)),
        pl.BlockSpec((d, tf), lambda i, j: (0, j)),
        pl.BlockSpec((d, tf), lambda i, j: (0, j)),
        pl.BlockSpec((tf, d), lambda i, j: (j, 0)),
    ]
    args = [x, gain.reshape(1, d), w1, w3, w2]
    if final_norm:
        in_specs.append(pl.BlockSpec((1, d), lambda i, j: (0, 0)))
        args.append(final_gain.reshape(1, d))
    return pl.pallas_call(
        functools.partial(_ffn_kernel, nf=nf, final_norm=final_norm),
        grid=(n // tm, nf),
        in_specs=in_specs,
        out_specs=pl.BlockSpec((tm, d), lambda i, j: (i, 0)),
        out_shape=jax.ShapeDtypeStruct((n, d), F32),
        scratch_shapes=[pltpu.VMEM((tm, d), BF16), pltpu.VMEM((tm, d), F32)],
        compiler_params=_cparams(("parallel", "arbitrary")),
        name="ffn",
    )(*args)


def _top2_gate(logits):
    lane = lax.broadcasted_iota(jnp.int32, logits.shape, 1)
    m1 = jnp.max(logits, axis=-1, keepdims=True)
    i1 = jnp.min(jnp.where(logits == m1, lane, LANES), axis=-1, keepdims=True)
    rest = jnp.where(lane == i1, -jnp.inf, logits)
    m2 = jnp.max(rest, axis=-1, keepdims=True)
    i2 = jnp.min(jnp.where(rest == m2, lane, LANES), axis=-1, keepdims=True)
    e2 = jnp.exp(m2 - m1)
    w1 = 1.0 / (1.0 + e2)
    w2 = e2 / (1.0 + e2)
    return jnp.where(lane == i1, w1, 0.0) + jnp.where(lane == i2, w2, 0.0)


def _moe_dense_kernel(x_ref, g_ref, r_ref, w1_ref, w3_ref, w2_ref, *rest, ne, nf, final_norm):
    if final_norm:
        fg_ref, o_ref, h_ref, acc_ref, gate_ref = rest
    else:
        fg_ref = None
        o_ref, h_ref, acc_ref, gate_ref = rest
    e = pl.program_id(1)
    j = pl.program_id(2)

    @pl.when((e == 0) & (j == 0))
    def _():
        hf = _rms(x_ref[...], g_ref[...])
        h_ref[...] = hf.astype(BF16)
        acc_ref[...] = jnp.zeros(acc_ref.shape, F32)
        logits = jnp.dot(hf, r_ref[...], preferred_element_type=F32, precision=lax.Precision.HIGHEST)
        lane = lax.broadcasted_iota(jnp.int32, logits.shape, 1)
        gate_ref[...] = _top2_gate(jnp.where(lane < N_EXPERTS, logits, -jnp.inf))

    lane = lax.broadcasted_iota(jnp.int32, gate_ref.shape, 1)
    ge = jnp.sum(jnp.where(lane == e, gate_ref[...], 0.0), axis=-1, keepdims=True)
    h = h_ref[...]
    a = jnp.dot(h, w1_ref[0], preferred_element_type=F32)
    b = jnp.dot(h, w3_ref[0], preferred_element_type=F32)
    t = (jax.nn.silu(a) * b * ge).astype(BF16)
    acc_ref[...] += jnp.dot(t, w2_ref[0], preferred_element_type=F32)

    @pl.when((e == ne - 1) & (j == nf - 1))
    def _():
        _finish(x_ref, acc_ref, o_ref, fg_ref)


def _moe_dense(x, gain, router_p, w1, w3, w2, final_gain=None, *, tm=512, tf=1408):
    n, d = x.shape
    ne, _, ff = w1.shape
    assert n % tm == 0 and ff % tf == 0
    nf = ff // tf
    final_norm = final_gain is not None
    in_specs = [
        pl.BlockSpec((tm, d), lambda i, e, j: (i, 0)),
        pl.BlockSpec((1, d), lambda i, e, j: (0, 0)),
        pl.BlockSpec((d, LANES), lambda i, e, j: (0, 0)),
        pl.BlockSpec((1, d, tf), lambda i, e, j: (e, 0, j)),
        pl.BlockSpec((1, d, tf), lambda i, e, j: (e, 0, j)),
        pl.BlockSpec((1, tf, d), lambda i, e, j: (e, j, 0)),
    ]
    args = [x, gain.reshape(1, d), router_p, w1, w3, w2]
    if final_norm:
        in_specs.append(pl.BlockSpec((1, d), lambda i, e, j: (0, 0)))
        args.append(final_gain.reshape(1, d))
    return pl.pallas_call(
        functools.partial(_moe_dense_kernel, ne=ne, nf=nf, final_norm=final_norm),
        grid=(n // tm, ne, nf),
        in_specs=in_specs,
        out_specs=pl.BlockSpec((tm, d), lambda i, e, j: (i, 0)),
        out_shape=jax.ShapeDtypeStruct((n, d), F32),
        scratch_shapes=[pltpu.VMEM((tm, d), BF16), pltpu.VMEM((tm, d), F32),
                        pltpu.VMEM((tm, LANES), F32)],
        compiler_params=_cparams(("parallel", "arbitrary", "arbitrary")),
        name="moe",
    )(*args)


def _t5_bucket(rel):
    nb = NUM_BUCKETS // 2
    max_exact = nb // 2
    ret = jnp.where(rel > 0, nb, 0)
    n = jnp.abs(rel)
    large = max_exact + (jnp.log(jnp.maximum(n, max_exact).astype(F32) / max_exact)
                         / math.log(MAX_DISTANCE / max_exact) * (nb - max_exact)).astype(jnp.int32)
    large = jnp.minimum(large, nb - 1)
    return ret + jnp.where(n < max_exact, n, large)


def _toeplitz(value_of_rel, rows, cols, col0):
    period = rows + cols
    k = np.arange(period)
    rel = jnp.asarray(col0 + np.where(k < cols, k, k - period), jnp.int32)
    w = value_of_rel(rel)
    lead = w.shape[:-1]
    flat = jnp.tile(w, (1,) * len(lead) + (rows,))[..., :rows * (period - 1)]
    return flat.reshape(lead + (rows, period - 1))[..., :cols]


def _rope_tables(seq):
    t = jnp.arange(seq)
    row = (t // GRID_W).astype(F32)
    col = (t % GRID_W).astype(F32)
    half = HEAD_DIM // 2
    inv = ROPE_THETA ** (-jnp.arange(0, half, 2, dtype=F32) / half)
    ar = row[:, None] * inv
    ac = col[:, None] * inv
    cos = jnp.concatenate([jnp.cos(ar), jnp.cos(ar), jnp.cos(ac), jnp.cos(ac)], axis=1)
    sin = jnp.concatenate([-jnp.sin(ar), jnp.sin(ar), -jnp.sin(ac), jnp.sin(ac)], axis=1)
    return jnp.tile(cos, (1, LANES // HEAD_DIM)), jnp.tile(sin, (1, LANES // HEAD_DIM))


def _bias_tiles_b(rel_bias, t):
    cols = rel_bias[:, :B_HEADS].astype(F32).T * LOG2E
    by_rel = lambda rel: cols[:, _t5_bucket(rel)]
    tiles = []
    for delta in (-2, -1, 0, 1, 2):
        if abs(delta) == 2:
            far = by_rel(jnp.full((1,), delta * t, jnp.int32))
            tiles.append(jnp.broadcast_to(far[:, :, None], (B_HEADS, t, t)))
        else:
            tiles.append(_toeplitz(lambda r: by_rel(-r), t, t, -delta * t))
    return jnp.stack(tiles, axis=1)


def _bias_tile_c(rel_bias, g, dil, span, tq, halo):
    cols = rel_bias[:, B_HEADS + g * C_HEADS:B_HEADS + (g + 1) * C_HEADS].astype(F32).T * LOG2E

    def by_rel(rel):
        return jnp.where((jnp.abs(rel) <= span)[None, :], cols[:, _t5_bucket(rel * dil)], NEG_BIG)

    return _toeplitz(by_rel, tq, tq + 2 * halo, -halo)


def _layout_w_in(w):
    a0 = 0
    b0 = a0 + A_SLAB
    c0 = b0 + 2 * B_QK + B_V
    g0 = c0 + 3 * C_QKV
    parts = [w[..., g0:g0 + GATE_COLS], w[..., a0:b0], w[..., b0:c0]]
    for g in range(N_CGROUPS):
        for kind in range(3):
            s = c0 + kind * C_QKV + g * C_OUT
            parts.append(w[..., s:s + C_OUT])
    return jnp.concatenate(parts, axis=-1)


def _trunk(x3, p):
    bsz, seq, d = x3.shape
    n = bsz * seq
    x = x3.reshape(n, d)
    depth = p["ln1"].shape[0]
    assert seq >= MAX_DISTANCE and depth > 0
    cos_t, sin_t = _rope_tables(seq)
    tb = min(512, seq)
    bias_b = _bias_tiles_b(p["rel_bias"], tb)
    win_halo = 128
    win_tq = [min(256, seq // dil) for _, dil in C_GROUPS]
    bias_c = []
    for g, (window, dil) in enumerate(C_GROUPS):
        span = window // (2 * dil)
        assert span <= win_halo
        bias_c.append(_bias_tile_c(p["rel_bias"], g, dil, span, win_tq[g], win_halo))

    for l in range(depth):
        y, *cgs = _norm_proj(x, p["ln1"][l], p["w_in"][l], p["colscale"], bsz, seq)
        qn, kd, vd = _prep_a(y, cos_t, sin_t, p["qg"][l], p["kg"][l], p["bd"], seq)
        oa = _flash_a(qn, kd, vd, bsz, seq)
        lambda_init = 0.8 - 0.6 * math.exp(-0.3 * l)
        ob = _flash_b(y, bias_b, p["lam"][l], p["subln"][l], bsz, seq, lambda_init, t=tb)
        ocs, lss = [], []
        for g, (window, dil) in enumerate(C_GROUPS):
            oc, ls = _window_group(cgs[g], bias_c[g], dil, bsz, seq, tq=win_tq[g], halo=win_halo)
            ocs.append(oc)
            lss.append(ls)
        x = _merge(x, oa, ob, ocs, lss, y, p["w_br"][l], p["w_o"][l])
        fg = p["ln_f"] if l == depth - 1 else None
        jj = l // 2
        if l % 2 == 0:
            x = _ffn(x, p["ln2"][l], p["ffn_w1"][jj], p["ffn_w3"][jj], p["ffn_w2"][jj], fg)
        else:
            x = _moe_dense(x, p["ln2"][l], p["router"][jj], p["exp_w1"][jj], p["exp_w3"][jj],
                           p["exp_w2"][jj], fg)
    return x.reshape(bsz, seq, d)


def kernel(x_prompt, x_sample, rel_bias, ln1, ln2, ln_f, w_in, a_qnorm, a_knorm, lam_q1, lam_k1,
           lam_q2, lam_k2, b_subln, w_br, w_o, ffn_w1, ffn_w3, ffn_w2, router, exp_w1, exp_w3, exp_w2):
    depth = ln1.shape[0]
    colscale = np.ones((1, IN_COLS), np.float32)
    colscale[:, OFF_BQ:OFF_BQ + B_QK] = QK_SCALE
    for g in range(N_CGROUPS):
        colscale[:, MAIN_COLS + g * C_SLAB:MAIN_COLS + g * C_SLAB + C_OUT] = QK_SCALE
    lam = (jnp.exp(jnp.sum(lam_q1.astype(F32) * lam_k1.astype(F32), axis=-1))
           - jnp.exp(jnp.sum(lam_q2.astype(F32) * lam_k2.astype(F32), axis=-1)))
    lam = lam + jnp.asarray([0.8 - 0.6 * math.exp(-0.3 * l) for l in range(depth)], F32)
    head_id = np.arange(LANES) // HEAD_DIM
    p = {
        "rel_bias": rel_bias,
        "ln1": ln1, "ln2": ln2, "ln_f": ln_f,
        "w_in": _layout_w_in(w_in).astype(BF16),
        "colscale": jnp.asarray(colscale),
        "qg": jnp.tile(a_qnorm.astype(F32), (1, LANES // HEAD_DIM)).reshape(depth, 1, LANES),
        "kg": jnp.tile(a_knorm.astype(F32), (1, LANES // HEAD_DIM)).reshape(depth, 1, LANES),
        "bd": jnp.asarray(head_id[:, None] == head_id[None, :], F32),
        "lam": lam.reshape(depth, 1),
        "subln": b_subln.astype(F32).reshape(depth, 1, 2 * HEAD_DIM),
        "w_br": w_br.astype(BF16), "w_o": w_o.astype(BF16),
        "ffn_w1": ffn_w1.astype(BF16), "ffn_w3": ffn_w3.astype(BF16), "ffn_w2": ffn_w2.astype(BF16),
        "router": jnp.pad(router.astype(F32), ((0, 0), (0, 0), (0, LANES - N_EXPERTS))),
        "exp_w1": exp_w1.astype(BF16), "exp_w3": exp_w3.astype(BF16), "exp_w2": exp_w2.astype(BF16),
    }
    return (_trunk(x_prompt, p), _trunk(x_sample, p))
```

```python
import functools
import math

import jax
import jax.numpy as jnp
import numpy as np
from jax import lax
from jax.experimental import pallas as pl
from jax.experimental.pallas import tpu as pltpu

F32 = jnp.float32
BF16 = jnp.bfloat16

D_MODEL = 1024
HEAD_DIM = 64
GRID_W = 64
ROPE_THETA = 10000.0
EPS = 1e-6
A_HEADS = 8
A_KV_HEADS = 2
B_HEADS = 4
C_GROUPS = ((128, 1), (512, 4), (2048, 16))
C_HEADS = 4
NUM_BUCKETS = 32
MAX_DISTANCE = 128
N_EXPERTS = 8
TOP_K = 2
N_BRANCH = 3

A_Q = A_HEADS * HEAD_DIM
A_KV = A_KV_HEADS * HEAD_DIM
B_QK = B_HEADS * 2 * HEAD_DIM
B_V = B_HEADS * 2 * HEAD_DIM
N_CGROUPS = len(C_GROUPS)
C_QKV = N_CGROUPS * C_HEADS * HEAD_DIM
C_OUT = C_HEADS * HEAD_DIM
GATE_COLS = N_BRANCH * D_MODEL
IN_COLS = A_Q + 2 * A_KV + 2 * B_QK + B_V + 3 * C_QKV + GATE_COLS

OFF_GATE = 0
OFF_AQ = GATE_COLS
A_SLAB = A_Q + 2 * A_KV
OFF_BQ = OFF_AQ + A_SLAB
OFF_BK = OFF_BQ + B_QK
OFF_BV = OFF_BK + B_QK
MAIN_COLS = OFF_BV + B_V
C_SLAB = 3 * C_OUT

LANES = 128
SUBLANES = 8
VMEM_LIMIT = 56 * 1024 * 1024
LOG2E = 1.4426950408889634
LN2 = 0.6931471805599453
QK_SCALE = HEAD_DIM ** -0.5 * LOG2E
NEG_BIG = -1e30

NT_DIMS = (((1,), (1,)), ((), ()))
TN_DIMS = (((0,), (0,)), ((), ()))


def _cparams(sem):
    return pltpu.CompilerParams(dimension_semantics=sem, vmem_limit_bytes=VMEM_LIMIT)


def _norm_proj_kernel(x_ref, g_ref, w_ref, cs_ref, o_ref, c0_ref, c1_ref, c2_ref, h_ref, ys_ref,
                      *, n_gate_tiles, n_main_tiles, dils):
    j = pl.program_id(1)
    tm = x_ref.shape[0]

    @pl.when(j == 0)
    def _():
        x = x_ref[...]
        ms = jnp.mean(x * x, axis=-1, keepdims=True)
        h_ref[...] = (x * lax.rsqrt(ms + EPS) * g_ref[...]).astype(BF16)

    y = jnp.dot(h_ref[...], w_ref[...], preferred_element_type=F32) * cs_ref[...]

    @pl.when(j < n_gate_tiles)
    def _():
        o_ref[...] = jax.nn.sigmoid(y).astype(BF16)

    @pl.when((j >= n_gate_tiles) & (j < n_main_tiles))
    def _():
        o_ref[...] = y.astype(BF16)

    for g, (c_ref, dil) in enumerate(zip((c0_ref, c1_ref, c2_ref), dils)):
        @pl.when(j == n_main_tiles + g)
        def _(c_ref=c_ref, dil=dil):
            if dil == 1:
                c_ref[0, 0] = y.astype(BF16)
            else:
                for c in range(y.shape[1] // LANES):
                    ys_ref[c] = y[:, c * LANES:(c + 1) * LANES]
                for r in range(dil):
                    for c in range(y.shape[1] // LANES):
                        c_ref[0, r, :, c * LANES:(c + 1) * LANES] = ys_ref[
                            c, pl.ds(r, tm // dil, stride=dil), :].astype(BF16)


def _norm_proj(x, gain, w, colscale, bsz, seq, *, tm=1024, tn=C_SLAB):
    n, d = x.shape
    tm = min(tm, seq)
    dils = tuple(dil for _, dil in C_GROUPS)
    assert seq % tm == 0 and tn == C_SLAB and MAIN_COLS % tn == 0 and GATE_COLS % tn == 0
    assert all(tm % (2 * SUBLANES * dil) == 0 for dil in dils) and len(dils) == 3
    n_main = MAIN_COLS // tn
    tps = seq // tm

    def c_spec(dil):
        return pl.BlockSpec((1, dil, tm // dil, tn), lambda i, j: (i // tps, 0, i % tps, 0))

    return pl.pallas_call(
        functools.partial(_norm_proj_kernel, n_gate_tiles=GATE_COLS // tn, n_main_tiles=n_main, dils=dils),
        grid=(n // tm, n_main + len(dils)),
        in_specs=[
            pl.BlockSpec((tm, d), lambda i, j: (i, 0)),
            pl.BlockSpec((1, d), lambda i, j: (0, 0)),
            pl.BlockSpec((d, tn), lambda i, j: (0, j)),
            pl.BlockSpec((1, tn), lambda i, j: (0, j)),
        ],
        out_specs=[pl.BlockSpec((tm, tn), lambda i, j: (i, jnp.minimum(j, n_main - 1)))]
        + [c_spec(dil) for dil in dils],
        out_shape=[jax.ShapeDtypeStruct((n, MAIN_COLS), BF16)]
        + [jax.ShapeDtypeStruct((bsz, dil, seq // dil, tn), BF16) for dil in dils],
        scratch_shapes=[pltpu.VMEM((tm, d), BF16), pltpu.VMEM((tn // LANES, tm, LANES), F32)],
        compiler_params=_cparams(("parallel", "arbitrary")),
        name="norm_proj",
    )(x, gain.reshape(1, d), w, colscale)


def _prep_a_kernel(y_ref, cos_ref, sin_ref, qg_ref, kg_ref, bd_ref, q_out, k_out, v_out):
    tm = y_ref.shape[0]
    lane = lax.broadcasted_iota(jnp.int32, (tm, LANES), 1)
    first16 = (lane % 32) < 16
    half0 = lane < HEAD_DIM
    cos = cos_ref[...]
    sin = sin_ref[...]
    bd = bd_ref[...]

    def norm_rope(xb, gain, out_scale):
        x = xb.astype(F32)
        ss = jnp.dot(x * x, bd, preferred_element_type=F32, precision=lax.Precision.HIGHEST)
        xn = x * lax.rsqrt(ss * (1.0 / HEAD_DIM) + EPS) * gain
        sw = jnp.where(first16, pltpu.roll(xn, LANES - 16, 1), pltpu.roll(xn, 16, 1))
        return (xn * cos + sw * sin) * out_scale

    qg = qg_ref[...]
    for c in range(A_Q // LANES):
        q_out[:, c * LANES:(c + 1) * LANES] = norm_rope(
            y_ref[:, c * LANES:(c + 1) * LANES], qg, QK_SCALE).astype(BF16)

    k = norm_rope(y_ref[:, A_Q:A_Q + LANES], kg_ref[...], 1.0)
    kr = pltpu.roll(k, HEAD_DIM, 1)
    k_out[:, 0:LANES] = jnp.where(half0, k, kr).astype(BF16)
    k_out[:, LANES:2 * LANES] = jnp.where(half0, kr, k).astype(BF16)

    v = y_ref[:, A_Q + LANES:A_Q + 2 * LANES].astype(F32)
    vr = pltpu.roll(v, HEAD_DIM, 1)
    v_out[:, 0:LANES] = jnp.where(half0, v, vr).astype(BF16)
    v_out[:, LANES:2 * LANES] = jnp.where(half0, vr, v).astype(BF16)


def _prep_a(y, cos_t, sin_t, qg, kg, bd, seq, *, tm=512):
    n = y.shape[0]
    assert OFF_AQ % A_SLAB == 0 and seq % tm == 0 and A_KV == LANES
    spt = seq // tm
    return pl.pallas_call(
        _prep_a_kernel,
        grid=(n // tm,),
        in_specs=[
            pl.BlockSpec((tm, A_SLAB), lambda i: (i, OFF_AQ // A_SLAB)),
            pl.BlockSpec((tm, LANES), lambda i: (i % spt, 0)),
            pl.BlockSpec((tm, LANES), lambda i: (i % spt, 0)),
            pl.BlockSpec((1, LANES), lambda i: (0, 0)),
            pl.BlockSpec((1, LANES), lambda i: (0, 0)),
            pl.BlockSpec((LANES, LANES), lambda i: (0, 0)),
        ],
        out_specs=[
            pl.BlockSpec((tm, A_Q), lambda i: (i, 0)),
            pl.BlockSpec((tm, 2 * LANES), lambda i: (i, 0)),
            pl.BlockSpec((tm, 2 * LANES), lambda i: (i, 0)),
        ],
        out_shape=[
            jax.ShapeDtypeStruct((n, A_Q), BF16),
            jax.ShapeDtypeStruct((n, 2 * LANES), BF16),
            jax.ShapeDtypeStruct((n, 2 * LANES), BF16),
        ],
        compiler_params=_cparams(("parallel",)),
        name="prep_a",
    )(y, cos_t, sin_t, qg, kg, bd)


def _colmax(s):
    tk, r = s.shape
    m8 = jnp.max(s.reshape(tk // SUBLANES, SUBLANES, r), axis=0)
    return jnp.max(m8, axis=0, keepdims=True)


def _softmax_pv(s, mb, v_ref, m_ref, l_ref, acc_ref):
    tk, r = s.shape
    m_prev = m_ref[...]
    m_cur = jnp.maximum(m_prev, mb)
    alpha = jnp.exp2(m_prev - m_cur)
    p = jnp.exp2(s - m_cur)
    l_ref[...] = alpha * l_ref[...] + jnp.sum(p.reshape(tk // SUBLANES, SUBLANES, r), axis=0)
    acc_ref[...] = alpha * acc_ref[...] + lax.dot_general(
        v_ref[...], p.astype(BF16), TN_DIMS, preferred_element_type=F32)
    m_ref[...] = m_cur


def _flash_steps(t, nk, scores_fn, init_fn, finish_fn, v_ref, s_ref, mb_ref, m_ref, l_ref, acc_ref):
    def produce(cur):
        for col, s_new in scores_fn():
            width = s_new.shape[1]
            mb_ref[cur, :, col:col + width] = _colmax(s_new)
            s_ref[cur, :, col:col + width] = s_new

    def consume(prev):
        _softmax_pv(s_ref[prev], mb_ref[prev], v_ref, m_ref, l_ref, acc_ref)

    @pl.when(t == 0)
    def _():
        m_ref[...] = jnp.full(m_ref.shape, NEG_BIG, F32)
        l_ref[...] = jnp.zeros(l_ref.shape, F32)
        acc_ref[...] = jnp.zeros(acc_ref.shape, F32)
        init_fn()
        produce(0)

    for par in (0, 1):
        @pl.when((t > 0) & (t < nk) & (t % 2 == par))
        def _(par=par):
            produce(par)
            consume(1 - par)

    @pl.when(t == nk)
    def _():
        consume((nk - 1) % 2)
        finish_fn()


def _flash_scratch(tk, r):
    return [
        pltpu.VMEM((r, LANES), BF16),
        pltpu.VMEM((2, tk, r), F32),
        pltpu.VMEM((2, 1, r), F32),
        pltpu.VMEM((1, r), F32),
        pltpu.VMEM((SUBLANES, r), F32),
        pltpu.VMEM((LANES, r), F32),
    ]


def _flash_a_kernel(q_ref, k_ref, v_ref, o_ref, qs_ref, s_ref, mb_ref, m_ref, l_ref, acc_ref, *, tq, nk):
    t = pl.program_id(3)
    grp = A_HEADS // A_KV_HEADS
    lane = lax.broadcasted_iota(jnp.int32, (tq, LANES), 1)
    half0 = lane < HEAD_DIM

    def init():
        for h in range(grp):
            c = h // 2
            x = q_ref[:, c * LANES:(c + 1) * LANES]
            keep = half0 if h % 2 == 0 else jnp.logical_not(half0)
            qs_ref[h * tq:(h + 1) * tq, :] = jnp.where(keep, x, jnp.zeros_like(x))

    def scores():
        yield 0, lax.dot_general(k_ref[...], qs_ref[...], NT_DIMS, preferred_element_type=F32)

    def finish():
        o_t = acc_ref[...] / jnp.sum(l_ref[...], axis=0, keepdims=True)
        for c in range(grp // 2):
            o0 = o_t[:, (2 * c) * tq:(2 * c + 1) * tq].T
            o1 = o_t[:, (2 * c + 1) * tq:(2 * c + 2) * tq].T
            o_ref[:, c * LANES:(c + 1) * LANES] = jnp.where(half0, o0, o1).astype(BF16)

    _flash_steps(t, nk, scores, init, finish, v_ref, s_ref, mb_ref, m_ref, l_ref, acc_ref)


def _flash_a(qn, kd, vd, bsz, seq, *, tq=512, tk=1024):
    n = qn.shape[0]
    tk = min(tk, seq)
    assert seq % tq == 0 and seq % tk == 0
    nq, nk = seq // tq, seq // tk
    grp = A_HEADS // A_KV_HEADS
    qw = grp * HEAD_DIM
    return pl.pallas_call(
        functools.partial(_flash_a_kernel, tq=tq, nk=nk),
        grid=(bsz, A_KV_HEADS, nq, nk + 1),
        in_specs=[
            pl.BlockSpec((tq, qw), lambda b, kv, qi, t: (b * nq + qi, kv)),
            pl.BlockSpec((tk, LANES), lambda b, kv, qi, t: (b * nk + jnp.minimum(t, nk - 1), kv)),
            pl.BlockSpec((tk, LANES), lambda b, kv, qi, t: (b * nk + jnp.maximum(t - 1, 0), kv)),
        ],
        out_specs=pl.BlockSpec((tq, qw), lambda b, kv, qi, t: (b * nq + qi, kv)),
        out_shape=jax.ShapeDtypeStruct((n, A_Q), BF16),
        scratch_shapes=_flash_scratch(tk, grp * tq),
        compiler_params=_cparams(("parallel", "parallel", "parallel", "arbitrary")),
        name="flash_a",
    )(qn, kd, vd)


def _flash_b_kernel(lam_ref, q_ref, k_ref, v_ref, bias_ref, sub_ref, o_ref,
                    qs_ref, s_ref, mb_ref, m_ref, l_ref, acc_ref, *, tq, nk, out_scale):
    qi = pl.program_id(2)
    t = pl.program_id(3)
    lane = lax.broadcasted_iota(jnp.int32, (tq, LANES), 1)
    half0 = lane < HEAD_DIM

    def init():
        x = q_ref[...]
        zero = jnp.zeros_like(x)
        qs_ref[0:tq, :] = jnp.where(half0, x, zero)
        qs_ref[tq:2 * tq, :] = jnp.where(half0, zero, x)

    def scores():
        bias = bias_ref[0, jnp.clip(t - qi, -2, 2) + 2]
        k = k_ref[...]
        for mp in range(2):
            yield mp * tq, lax.dot_general(
                k, qs_ref[mp * tq:(mp + 1) * tq, :], NT_DIMS, preferred_element_type=F32) + bias

    def finish():
        lam = lam_ref[0]
        o_t = acc_ref[...] / jnp.sum(l_ref[...], axis=0, keepdims=True)
        o = (o_t[:, 0:tq] - lam * o_t[:, tq:2 * tq]).T
        ms = jnp.mean(o * o, axis=-1, keepdims=True)
        o_ref[...] = (o * lax.rsqrt(ms + EPS) * sub_ref[...] * out_scale).astype(BF16)

    _flash_steps(t, nk, scores, init, finish, v_ref, s_ref, mb_ref, m_ref, l_ref, acc_ref)


def _flash_b(y, bias_tiles, lam, subln, bsz, seq, lambda_init, *, t):
    n = y.shape[0]
    assert seq % t == 0 and bias_tiles.shape == (B_HEADS, 5, t, t)
    nq = seq // t
    qb, kb, vb = OFF_BQ // LANES, OFF_BK // LANES, OFF_BV // LANES
    return pl.pallas_call(
        functools.partial(_flash_b_kernel, tq=t, nk=nq, out_scale=1.0 - lambda_init),
        grid=(B_HEADS, bsz, nq, nq + 1),
        in_specs=[
            pl.BlockSpec(memory_space=pltpu.SMEM),
            pl.BlockSpec((t, LANES), lambda h, b, qi, ts: (b * nq + qi, qb + h)),
            pl.BlockSpec((t, LANES), lambda h, b, qi, ts: (b * nq + jnp.minimum(ts, nq - 1), kb + h)),
            pl.BlockSpec((t, LANES), lambda h, b, qi, ts: (b * nq + jnp.maximum(ts - 1, 0), vb + h)),
            pl.BlockSpec((1, 5, t, t), lambda h, b, qi, ts: (h, 0, 0, 0), pipeline_mode=pl.Buffered(1)),
            pl.BlockSpec((1, LANES), lambda h, b, qi, ts: (0, 0)),
        ],
        out_specs=pl.BlockSpec((t, LANES), lambda h, b, qi, ts: (b * nq + qi, h)),
        out_shape=jax.ShapeDtypeStruct((n, B_V), BF16),
        scratch_shapes=_flash_scratch(t, 2 * t),
        compiler_params=_cparams(("parallel", "parallel", "parallel", "arbitrary")),
        name="flash_b",
    )(lam, y, y, y, bias_tiles, subln)


def _window_kernel(q_ref, kp_ref, km_ref, kn_ref, vp_ref, vm_ref, vn_ref, bias_ref,
                   o_ref, lse_ref, *, tq, halo, length, dil):
    i = pl.program_id(1)
    nkeys = tq + 2 * halo
    ukey = i * tq - halo + lax.broadcasted_iota(jnp.int32, (nkeys, 2 * tq), 0)
    key_ok = (ukey >= 0) & (ukey < length)
    lane = lax.broadcasted_iota(jnp.int32, (tq, LANES), 1)
    half0 = lane < HEAD_DIM
    top_half = lax.broadcasted_iota(jnp.int32, (LANES, tq), 0) < HEAD_DIM

    def one_residue(r, rows, edge):
        q = q_ref[0, r]
        kcat = jnp.concatenate([kp_ref[0, r], km_ref[0, r], kn_ref[0, r]], axis=0)
        vcat = jnp.concatenate([vp_ref[0, r], vm_ref[0, r], vn_ref[0, r]], axis=0)
        for c in range(C_HEADS // 2):
            qc = q[:, c * LANES:(c + 1) * LANES]
            kc = kcat[:, c * LANES:(c + 1) * LANES]
            vc = vcat[:, c * LANES:(c + 1) * LANES]
            zero = jnp.zeros_like(qc)
            qpair = jnp.concatenate([jnp.where(half0, qc, zero), jnp.where(half0, zero, qc)], axis=0)
            s = lax.dot_general(kc, qpair, NT_DIMS, preferred_element_type=F32)
            s = s + bias_ref[c]
            if edge:
                s = jnp.where(key_ok, s, NEG_BIG)
            m = _colmax(s)
            e = jnp.exp2(s - m)
            den = jnp.sum(e.reshape(nkeys // SUBLANES, SUBLANES, 2 * tq), axis=0)
            den = jnp.sum(den, axis=0, keepdims=True)
            o_t = lax.dot_general(vc, e.astype(BF16), TN_DIMS, preferred_element_type=F32) / den
            lse = (m + jnp.log2(den)) * LN2
            o_pair = jnp.where(top_half, o_t[:, 0:tq], o_t[:, tq:2 * tq])
            lse_pair = jnp.where(top_half, lse[:, 0:tq], lse[:, tq:2 * tq])
            o_ref[c, rows, :] = o_pair.T
            lse_ref[c, rows, :] = lse_pair.T

    def all_residues(edge):
        if dil == 1:
            one_residue(0, pl.ds(0, tq), edge)
        else:
            def body(r, carry):
                one_residue(r, pl.ds(r, tq, stride=dil), edge)
                return carry
            lax.fori_loop(0, dil, body, 0, unroll=2)

    on_edge = (i == 0) | (i == pl.num_programs(1) - 1)

    @pl.when(on_edge)
    def _():
        all_residues(True)

    @pl.when(jnp.logical_not(on_edge))
    def _():
        all_residues(False)


def _window_group(cg, bias_tile, dil, bsz, seq, *, tq, halo):
    length = seq // dil
    tq = min(tq, length)
    cw = C_OUT
    assert length % tq == 0 and tq % halo == 0 and length % halo == 0
    assert bias_tile.shape == (C_HEADS // 2, tq + 2 * halo, 2 * tq)
    nt = length // tq
    hb = tq // halo
    nhb = length // halo
    main = lambda col: pl.BlockSpec((1, dil, tq, cw), lambda b, i: (b, 0, i, col))
    prev = lambda col: pl.BlockSpec(
        (1, dil, halo, cw), lambda b, i: (b, 0, jnp.maximum(i * hb - 1, 0), col))
    nxt = lambda col: pl.BlockSpec(
        (1, dil, halo, cw), lambda b, i: (b, 0, jnp.minimum((i + 1) * hb, nhb - 1), col))
    out = pl.BlockSpec((cw // LANES, tq * dil, LANES), lambda b, i: (0, b * nt + i, 0))
    return pl.pallas_call(
        functools.partial(_window_kernel, tq=tq, halo=halo, length=length, dil=dil),
        grid=(bsz, nt),
        in_specs=[main(0), prev(1), main(1), nxt(1), prev(2), main(2), nxt(2),
                  pl.BlockSpec(bias_tile.shape, lambda b, i: (0, 0, 0))],
        out_specs=[out, out],
        out_shape=[jax.ShapeDtypeStruct((cw // LANES, bsz * seq, LANES), F32)] * 2,
        compiler_params=_cparams(("parallel", "parallel")),
        name=f"window_d{dil}",
    )(cg, cg, cg, cg, cg, cg, cg, bias_tile)


def _merge_kernel(x_ref, oa_ref, ob_ref, oc0_ref, oc1_ref, oc2_ref, ls0_ref, ls1_ref, ls2_ref,
                  g0_ref, g1_ref, g2_ref, wb_ref, wo_ref, o_ref):
    planes = lambda ref: jnp.concatenate([ref[c] for c in range(C_OUT // LANES)], axis=1)
    ls = [planes(ls0_ref), planes(ls1_ref), planes(ls2_ref)]
    ocs = [planes(oc0_ref), planes(oc1_ref), planes(oc2_ref)]
    m = jnp.maximum(jnp.maximum(ls[0], ls[1]), ls[2])
    es = [jnp.exp(l - m) for l in ls]
    den = es[0] + es[1] + es[2]
    oc = (es[0] * ocs[0] + es[1] * ocs[1] + es[2] * ocs[2]) / den
    ya = jnp.dot(oa_ref[...], wb_ref[0:A_Q, :], preferred_element_type=F32)
    yb = jnp.dot(ob_ref[...], wb_ref[A_Q:A_Q + B_V, :], preferred_element_type=F32)
    yc = jnp.dot(oc.astype(BF16), wb_ref[A_Q + B_V:, :], preferred_element_type=F32)
    merged = (g0_ref[...].astype(F32) * ya + g1_ref[...].astype(F32) * yb
              + g2_ref[...].astype(F32) * yc)
    o_ref[...] = x_ref[...] + jnp.dot(merged.astype(BF16), wo_ref[...], preferred_element_type=F32)


def _merge(x, oa, ob, ocs, lss, y, wb, wo, *, tm=512):
    n, d = x.shape
    assert n % tm == 0 and OFF_GATE == 0
    row = lambda w: pl.BlockSpec((tm, w), lambda i: (i, 0))
    gate = lambda j: pl.BlockSpec((tm, d), lambda i: (i, j))
    plane = pl.BlockSpec((C_OUT // LANES, tm, LANES), lambda i: (0, i, 0))
    return pl.pallas_call(
        _merge_kernel,
        grid=(n // tm,),
        in_specs=[row(d), row(A_Q), row(B_V)] + [plane] * 6 + [gate(0), gate(1), gate(2)] + [
            pl.BlockSpec(wb.shape, lambda i: (0, 0)),
            pl.BlockSpec(wo.shape, lambda i: (0, 0)),
        ],
        out_specs=row(d),
        out_shape=jax.ShapeDtypeStruct((n, d), F32),
        compiler_params=_cparams(("parallel",)),
        name="merge",
    )(x, oa, ob, *ocs, *lss, y, y, y, wb, wo)


def _rms(x, gain):
    ms = jnp.mean(x * x, axis=-1, keepdims=True)
    return x * lax.rsqrt(ms + EPS) * gain


def _finish(x_ref, acc_ref, o_ref, fg_ref):
    out = x_ref[...] + acc_ref[...]
    if fg_ref is not None:
        out = _rms(out, fg_ref[...])
    o_ref[...] = out


def _ffn_kernel(x_ref, g_ref, w1_ref, w3_ref, w2_ref, *rest, nf, final_norm):
    if final_norm:
        fg_ref, o_ref, h_ref, acc_ref = rest
    else:
        fg_ref = None
        o_ref, h_ref, acc_ref = rest
    j = pl.program_id(1)

    @pl.when(j == 0)
    def _():
        h_ref[...] = _rms(x_ref[...], g_ref[...]).astype(BF16)
        acc_ref[...] = jnp.zeros(acc_ref.shape, F32)

    h = h_ref[...]
    a = jnp.dot(h, w1_ref[...], preferred_element_type=F32)
    b = jnp.dot(h, w3_ref[...], preferred_element_type=F32)
    t = (jax.nn.silu(a) * b).astype(BF16)
    acc_ref[...] += jnp.dot(t, w2_ref[...], preferred_element_type=F32)

    @pl.when(j == nf - 1)
    def _():
        _finish(x_ref, acc_ref, o_ref, fg_ref)


def _ffn(x, gain, w1, w3, w2, final_gain=None, *, tm=512, tf=1408):
    n, d = x.shape
    ff = w1.shape[1]
    assert n % tm == 0 and ff % tf == 0
    nf = ff // tf
    final_norm = final_gain is not None
    in_specs = [
        pl.BlockSpec((tm, d), lambda i, j: (i, 0)),
        pl.BlockSpec((1, d), lambda i, j: (0, 0)),
        pl.BlockSpec((d, tf), lambda i, j: (0, j)),
        pl.BlockSpec((d, tf), lambda i, j: (0, j)),
        pl.BlockSpec((tf, d), lambda i, j: (j, 0)),
    ]
    args = [x, gain.reshape(1, d), w1, w3, w2]
    if final_norm:
        in_specs.append(pl.BlockSpec((1, d), lambda i, j: (0, 0)))
        args.append(final_gain.reshape(1, d))
    return pl.pallas_call(
        functools.partial(_ffn_kernel, nf=nf, final_norm=final_norm),
        grid=(n // tm, nf),
        in_specs=in_specs,
        out_specs=pl.BlockSpec((tm, d), lambda i, j: (i, 0)),
        out_shape=jax.ShapeDtypeStruct((n, d), F32),
        scratch_shapes=[pltpu.VMEM((tm, d), BF16), pltpu.VMEM((tm, d), F32)],
        compiler_params=_cparams(("parallel", "arbitrary")),
        name="ffn",
    )(*args)


def _top2_gate(logits):
    lane = lax.broadcasted_iota(jnp.int32, logits.shape, 1)
    m1 = jnp.max(logits, axis=-1, keepdims=True)
    i1 = jnp.min(jnp.where(logits == m1, lane, LANES), axis=-1, keepdims=True)
    rest = jnp.where(lane == i1, -jnp.inf, logits)
    m2 = jnp.max(rest, axis=-1, keepdims=True)
    i2 = jnp.min(jnp.where(rest == m2, lane, LANES), axis=-1, keepdims=True)
    e2 = jnp.exp(m2 - m1)
    w1 = 1.0 / (1.0 + e2)
    w2 = e2 / (1.0 + e2)
    return jnp.where(lane == i1, w1, 0.0) + jnp.where(lane == i2, w2, 0.0)


def _moe_route_kernel(x_ref, g_ref, r_ref, tri_ref, cum_ref, h_ref, gate_ref, rank_ref, mask_ref, cnt_ref):
    hf = _rms(x_ref[...], g_ref[...])
    h_ref[...] = hf.astype(BF16)
    logits = jnp.dot(hf, r_ref[...], preferred_element_type=F32, precision=lax.Precision.HIGHEST)
    lane = lax.broadcasted_iota(jnp.int32, logits.shape, 1)
    gate = _top2_gate(jnp.where(lane < N_EXPERTS, logits, -jnp.inf))
    gate_ref[...] = gate
    sel = jnp.where(gate.T[0:N_EXPERTS, :] != 0.0, 1.0, 0.0)
    mask_ref[0] = sel
    selb = sel.astype(BF16)
    rank_ref[0] = jnp.dot(selb, tri_ref[...], preferred_element_type=F32)
    cnt_ref[0] = jnp.dot(selb, cum_ref[...], preferred_element_type=F32).astype(jnp.int32)


def _moe_route(x, gain, router_p, tri, cum, *, tm):
    n, d = x.shape
    nt = n // tm
    full = lambda shape: pl.BlockSpec(shape, lambda i: (0,) * len(shape))
    per_tile = lambda shape: pl.BlockSpec((1,) + shape, lambda i: (i, 0, 0))
    return pl.pallas_call(
        _moe_route_kernel,
        grid=(nt,),
        in_specs=[pl.BlockSpec((tm, d), lambda i: (i, 0)), full((1, d)), full((d, LANES)),
                  full((tm, tm)), full((tm, LANES))],
        out_specs=[pl.BlockSpec((tm, d), lambda i: (i, 0)), pl.BlockSpec((tm, LANES), lambda i: (i, 0)),
                   per_tile((N_EXPERTS, tm)), per_tile((N_EXPERTS, tm)), per_tile((N_EXPERTS, LANES))],
        out_shape=[jax.ShapeDtypeStruct((n, d), BF16), jax.ShapeDtypeStruct((n, LANES), F32),
                   jax.ShapeDtypeStruct((nt, N_EXPERTS, tm), F32),
                   jax.ShapeDtypeStruct((nt, N_EXPERTS, tm), F32),
                   jax.ShapeDtypeStruct((nt, N_EXPERTS, LANES), jnp.int32)],
        compiler_params=_cparams(("parallel",)),
        name="moe_route",
    )(x, gain.reshape(1, d), router_p, tri, cum)


def _moe_experts_kernel(cnt_ref, h_ref, gate_ref, rank_ref, mask_ref, w1_ref, w3_ref, w2_ref, o_ref,
                        hc_ref, oc_ref, *, nf, nb, ch, wb, fb):
    i, e, j = pl.program_id(0), pl.program_id(1), pl.program_id(2)
    tm, d = h_ref.shape
    n_rows = cnt_ref[i, e, nb]
    n_chunks = (n_rows + ch - 1) // ch

    def onehot(c, b):
        rank = rank_ref[0, pl.ds(e, 1), b * wb:(b + 1) * wb]
        sel = mask_ref[0, pl.ds(e, 1), b * wb:(b + 1) * wb]
        srow = (c * ch + lax.broadcasted_iota(jnp.int32, (ch, 1), 0)).astype(F32)
        return jnp.where((rank == srow) & (sel > 0.5), 1.0, 0.0).astype(BF16)

    def hits(c, b):
        return (cnt_ref[i, e, b] < (c + 1) * ch) & (cnt_ref[i, e, b + 1] > c * ch)

    @pl.when((e == 0) & (j == 0))
    def _():
        o_ref[...] = jnp.zeros(o_ref.shape, F32)

    @pl.when(j == 0)
    def _():
        def gather(c, carry):
            rows = pl.ds(pl.multiple_of(c * ch, ch), ch)
            hc_ref[rows, :] = jnp.zeros((ch, d), BF16)
            for b in range(nb):
                @pl.when(hits(c, b))
                def _(b=b):
                    picked = jnp.dot(onehot(c, b), h_ref[b * wb:(b + 1) * wb, :], preferred_element_type=F32)
                    hc_ref[rows, :] += picked.astype(BF16)
            return carry
        lax.fori_loop(0, n_chunks, gather, 0)

        @pl.when(n_chunks * ch < tm)
        def _():
            hc_ref[pl.ds(pl.multiple_of(n_chunks * ch, ch), ch), :] = jnp.zeros((ch, d), BF16)

    def expert(f, carry):
        rows = pl.ds(pl.multiple_of(f * fb, fb), fb)
        hcb = hc_ref[rows, :]
        a = jnp.dot(hcb, w1_ref[0], preferred_element_type=F32)
        b3 = jnp.dot(hcb, w3_ref[0], preferred_element_type=F32)
        r = jnp.dot((jax.nn.silu(a) * b3).astype(BF16), w2_ref[0], preferred_element_type=F32)

        @pl.when(j == 0)
        def _():
            oc_ref[rows, :] = r

        @pl.when(j > 0)
        def _():
            oc_ref[rows, :] += r
        return carry
    lax.fori_loop(0, (n_rows + fb - 1) // fb, expert, 0)

    @pl.when(j == nf - 1)
    def _():
        def scatter(c, carry):
            ocb = oc_ref[pl.ds(pl.multiple_of(c * ch, ch), ch), :].astype(BF16)
            for b in range(nb):
                @pl.when(hits(c, b))
                def _(b=b):
                    back = lax.dot_general(onehot(c, b), ocb, TN_DIMS, preferred_element_type=F32)
                    g = gate_ref[b * wb:(b + 1) * wb, :]
                    lane = lax.broadcasted_iota(jnp.int32, g.shape, 1)
                    gcol = jnp.sum(jnp.where(lane == e, g, 0.0), axis=-1, keepdims=True)
                    o_ref[b * wb:(b + 1) * wb, :] += gcol * back
            return carry
        lax.fori_loop(0, n_chunks, scatter, 0)


def _moe_experts(cnt, h, gate, rank, mask, w1, w3, w2, *, tm, tf=1408, ch=128, wb=256, fb=256):
    n, d = h.shape
    ne, _, ff = w1.shape
    assert n % tm == 0 and ff % tf == 0 and tm % wb == 0 and tm % fb == 0 and fb % ch == 0
    nf, nb = ff // tf, tm // wb
    assert cnt.shape == (n // tm, ne, nb + 1)
    grid_spec = pltpu.PrefetchScalarGridSpec(
        num_scalar_prefetch=1,
        grid=(n // tm, ne, nf),
        in_specs=[
            pl.BlockSpec((tm, d), lambda i, e, j, c: (i, 0), pipeline_mode=pl.Buffered(1)),
            pl.BlockSpec((tm, LANES), lambda i, e, j, c: (i, 0)),
            pl.BlockSpec((1, ne, tm), lambda i, e, j, c: (i, 0, 0)),
            pl.BlockSpec((1, ne, tm), lambda i, e, j, c: (i, 0, 0)),
            pl.BlockSpec((1, d, tf), lambda i, e, j, c: (e, 0, j)),
            pl.BlockSpec((1, d, tf), lambda i, e, j, c: (e, 0, j)),
            pl.BlockSpec((1, tf, d), lambda i, e, j, c: (e, j, 0)),
        ],
        out_specs=pl.BlockSpec((tm, d), lambda i, e, j, c: (i, 0), pipeline_mode=pl.Buffered(1)),
        scratch_shapes=[pltpu.VMEM((tm, d), BF16), pltpu.VMEM((tm, d), F32)],
    )
    return pl.pallas_call(
        functools.partial(_moe_experts_kernel, nf=nf, nb=nb, ch=ch, wb=wb, fb=fb),
        grid_spec=grid_spec,
        out_shape=jax.ShapeDtypeStruct((n, d), F32),
        compiler_params=_cparams(("parallel", "arbitrary", "arbitrary")),
        name="moe_experts",
    )(cnt, h, gate, rank, mask, w1, w3, w2)


def _residual_kernel(x_ref, y_ref, *rest, final_norm):
    if final_norm:
        fg_ref, o_ref = rest
    else:
        fg_ref = None
        (o_ref,) = rest
    _finish(x_ref, y_ref, o_ref, fg_ref)


def _residual(x, y, final_gain=None, *, tm=1024):
    n, d = x.shape
    assert n % tm == 0
    final_norm = final_gain is not None
    row = pl.BlockSpec((tm, d), lambda i: (i, 0))
    in_specs, args = [row, row], [x, y]
    if final_norm:
        in_specs.append(pl.BlockSpec((1, d), lambda i: (0, 0)))
        args.append(final_gain.reshape(1, d))
    return pl.pallas_call(
        functools.partial(_residual_kernel, final_norm=final_norm),
        grid=(n // tm,),
        in_specs=in_specs,
        out_specs=row,
        out_shape=jax.ShapeDtypeStruct((n, d), F32),
        compiler_params=_cparams(("parallel",)),
        name="residual",
    )(*args)


def _moe(x, gain, router_p, w1, w3, w2, final_gain=None, *, tm=2048, wb=256):
    n = x.shape[0]
    tm = min(tm, n)
    t = jnp.arange(tm)
    tri = (t[:, None] < t[None, :]).astype(BF16)
    cum = (t[:, None] < jnp.arange(LANES)[None, :] * wb).astype(BF16)
    h, gate, rank, mask, cnt = _moe_route(x, gain, router_p, tri, cum, tm=tm)
    y = _moe_experts(cnt[:, :, :tm // wb + 1], h, gate, rank, mask, w1, w3, w2, tm=tm, wb=wb)
    return _residual(x, y, final_gain)


def _t5_bucket(rel):
    nb = NUM_BUCKETS // 2
    max_exact = nb // 2
    ret = jnp.where(rel > 0, nb, 0)
    n = jnp.abs(rel)
    large = max_exact + (jnp.log(jnp.maximum(n, max_exact).astype(F32) / max_exact)
                         / math.log(MAX_DISTANCE / max_exact) * (nb - max_exact)).astype(jnp.int32)
    large = jnp.minimum(large, nb - 1)
    return ret + jnp.where(n < max_exact, n, large)


def _toeplitz(value_of_rel, rows, cols, col0):
    period = rows + cols
    k = np.arange(period)
    rel = jnp.asarray(col0 + np.where(k < cols, k, k - period), jnp.int32)
    w = value_of_rel(rel)
    lead = w.shape[:-1]
    flat = jnp.tile(w, (1,) * len(lead) + (rows,))[..., :rows * (period - 1)]
    return flat.reshape(lead + (rows, period - 1))[..., :cols]


def _rope_tables(seq):
    t = jnp.arange(seq)
    row = (t // GRID_W).astype(F32)
    col = (t % GRID_W).astype(F32)
    half = HEAD_DIM // 2
    inv = ROPE_THETA ** (-jnp.arange(0, half, 2, dtype=F32) / half)
    ar = row[:, None] * inv
    ac = col[:, None] * inv
    cos = jnp.concatenate([jnp.cos(ar), jnp.cos(ar), jnp.cos(ac), jnp.cos(ac)], axis=1)
    sin = jnp.concatenate([-jnp.sin(ar), jnp.sin(ar), -jnp.sin(ac), jnp.sin(ac)], axis=1)
    return jnp.tile(cos, (1, LANES // HEAD_DIM)), jnp.tile(sin, (1, LANES // HEAD_DIM))


def _bias_tiles_b(rel_bias, t):
    cols = rel_bias[:, :B_HEADS].astype(F32).T * LOG2E
    by_rel = lambda rel: cols[:, _t5_bucket(rel)]
    tiles = []
    for delta in (-2, -1, 0, 1, 2):
        if abs(delta) == 2:
            far = by_rel(jnp.full((1,), delta * t, jnp.int32))
            tiles.append(jnp.broadcast_to(far[:, :, None], (B_HEADS, t, t)))
        else:
            tiles.append(_toeplitz(lambda r: by_rel(-r), t, t, -delta * t))
    return jnp.stack(tiles, axis=1)


def _bias_tile_c(rel_bias, g, dil, span, tq, halo):
    cols = rel_bias[:, B_HEADS + g * C_HEADS:B_HEADS + (g + 1) * C_HEADS].astype(F32).T * LOG2E
    nkeys = tq + 2 * halo

    def by_rel(rel):
        return jnp.where((jnp.abs(rel) <= span)[None, :], cols[:, _t5_bucket(rel * dil)], NEG_BIG)

    per_head = _toeplitz(lambda r: by_rel(-r), nkeys, tq, halo)
    pairs = per_head.reshape(C_HEADS // 2, 2, nkeys, tq).transpose(0, 2, 1, 3)
    return pairs.reshape(C_HEADS // 2, nkeys, 2 * tq)


def _layout_w_in(w):
    a0 = 0
    b0 = a0 + A_SLAB
    c0 = b0 + 2 * B_QK + B_V
    g0 = c0 + 3 * C_QKV
    parts = [w[..., g0:g0 + GATE_COLS], w[..., a0:b0], w[..., b0:c0]]
    for g in range(N_CGROUPS):
        for kind in range(3):
            s = c0 + kind * C_QKV + g * C_OUT
            parts.append(w[..., s:s + C_OUT])
    return jnp.concatenate(parts, axis=-1)


def _trunk(x3, p):
    bsz, seq, d = x3.shape
    n = bsz * seq
    x = x3.reshape(n, d)
    depth = p["ln1"].shape[0]
    assert seq >= MAX_DISTANCE and depth > 0
    cos_t, sin_t = _rope_tables(seq)
    tb = min(1024, seq)
    bias_b = _bias_tiles_b(p["rel_bias"], tb)
    win_halo = 64
    win_tq = [min(512, seq // dil, 4096 // dil) for _, dil in C_GROUPS]
    bias_c = []
    for g, (window, dil) in enumerate(C_GROUPS):
        span = window // (2 * dil)
        assert span <= win_halo
        bias_c.append(_bias_tile_c(p["rel_bias"], g, dil, span, win_tq[g], win_halo))

    for l in range(depth):
        y, *cgs = _norm_proj(x, p["ln1"][l], p["w_in"][l], p["colscale"], bsz, seq)
        qn, kd, vd = _prep_a(y, cos_t, sin_t, p["qg"][l], p["kg"][l], p["bd"], seq)
        oa = _flash_a(qn, kd, vd, bsz, seq)
        lambda_init = 0.8 - 0.6 * math.exp(-0.3 * l)
        ob = _flash_b(y, bias_b, p["lam"][l], p["subln"][l], bsz, seq, lambda_init, t=tb)
        ocs, lss = [], []
        for g, (window, dil) in enumerate(C_GROUPS):
            oc, ls = _window_group(cgs[g], bias_c[g], dil, bsz, seq, tq=win_tq[g], halo=win_halo)
            ocs.append(oc)
            lss.append(ls)
        x = _merge(x, oa, ob, ocs, lss, y, p["w_br"][l], p["w_o"][l])
        fg = p["ln_f"] if l == depth - 1 else None
        jj = l // 2
        if l % 2 == 0:
            x = _ffn(x, p["ln2"][l], p["ffn_w1"][jj], p["ffn_w3"][jj], p["ffn_w2"][jj], fg)
        else:
            x = _moe(x, p["ln2"][l], p["router"][jj], p["exp_w1"][jj], p["exp_w3"][jj],
                     p["exp_w2"][jj], fg)
    return x.reshape(bsz, seq, d)


def kernel(x_prompt, x_sample, rel_bias, ln1, ln2, ln_f, w_in, a_qnorm, a_knorm, lam_q1, lam_k1,
           lam_q2, lam_k2, b_subln, w_br, w_o, ffn_w1, ffn_w3, ffn_w2, router, exp_w1, exp_w3, exp_w2):
    depth = ln1.shape[0]
    colscale = np.ones((1, IN_COLS), np.float32)
    colscale[:, OFF_BQ:OFF_BQ + B_QK] = QK_SCALE
    for g in range(N_CGROUPS):
        colscale[:, MAIN_COLS + g * C_SLAB:MAIN_COLS + g * C_SLAB + C_OUT] = QK_SCALE
    lam = (jnp.exp(jnp.sum(lam_q1.astype(F32) * lam_k1.astype(F32), axis=-1))
           - jnp.exp(jnp.sum(lam_q2.astype(F32) * lam_k2.astype(F32), axis=-1)))
    lam = lam + jnp.asarray([0.8 - 0.6 * math.exp(-0.3 * l) for l in range(depth)], F32)
    head_id = np.arange(LANES) // HEAD_DIM
    p = {
        "rel_bias": rel_bias,
        "ln1": ln1, "ln2": ln2, "ln_f": ln_f,
        "w_in": _layout_w_in(w_in).astype(BF16),
        "colscale": jnp.asarray(colscale),
        "qg": jnp.tile(a_qnorm.astype(F32), (1, LANES // HEAD_DIM)).reshape(depth, 1, LANES),
        "kg": jnp.tile(a_knorm.astype(F32), (1, LANES // HEAD_DIM)).reshape(depth, 1, LANES),
        "bd": jnp.asarray(head_id[:, None] == head_id[None, :], F32),
        "lam": lam.reshape(depth, 1),
        "subln": b_subln.astype(F32).reshape(depth, 1, 2 * HEAD_DIM),
        "w_br": w_br.astype(BF16), "w_o": w_o.astype(BF16),
        "ffn_w1": ffn_w1.astype(BF16), "ffn_w3": ffn_w3.astype(BF16), "ffn_w2": ffn_w2.astype(BF16),
        "router": jnp.pad(router.astype(F32), ((0, 0), (0, 0), (0, LANES - N_EXPERTS))),
        "exp_w1": exp_w1.astype(BF16), "exp_w3": exp_w3.astype(BF16), "exp_w2": exp_w2.astype(BF16),
    }
    return (_trunk(x_prompt, p), _trunk(x_sample, p))
```

```python
import functools
import math

import jax
import jax.numpy as jnp
import numpy as np
from jax import lax
from jax.experimental import pallas as pl
from jax.experimental.pallas import tpu as pltpu

F32 = jnp.float32
BF16 = jnp.bfloat16

D_MODEL = 1024
HEAD_DIM = 64
GRID_W = 64
ROPE_THETA = 10000.0
EPS = 1e-6
A_HEADS = 8
A_KV_HEADS = 2
B_HEADS = 4
C_GROUPS = ((128, 1), (512, 4), (2048, 16))
C_HEADS = 4
NUM_BUCKETS = 32
MAX_DISTANCE = 128
N_EXPERTS = 8
TOP_K = 2
N_BRANCH = 3

A_Q = A_HEADS * HEAD_DIM
A_KV = A_KV_HEADS * HEAD_DIM
B_QK = B_HEADS * 2 * HEAD_DIM
B_V = B_HEADS * 2 * HEAD_DIM
N_CGROUPS = len(C_GROUPS)
C_QKV = N_CGROUPS * C_HEADS * HEAD_DIM
C_OUT = C_HEADS * HEAD_DIM
GATE_COLS = N_BRANCH * D_MODEL
IN_COLS = A_Q + 2 * A_KV + 2 * B_QK + B_V + 3 * C_QKV + GATE_COLS

OFF_GATE = 0
OFF_AQ = GATE_COLS
A_SLAB = A_Q + 2 * A_KV
OFF_BQ = OFF_AQ + A_SLAB
OFF_BK = OFF_BQ + B_QK
OFF_BV = OFF_BK + B_QK
MAIN_COLS = OFF_BV + B_V
C_SLAB = 3 * C_OUT

LANES = 128
SUBLANES = 8
VMEM_LIMIT = 56 * 1024 * 1024
LOG2E = 1.4426950408889634
LN2 = 0.6931471805599453
QK_SCALE = HEAD_DIM ** -0.5 * LOG2E
NEG_BIG = -1e30

NT_DIMS = (((1,), (1,)), ((), ()))
TN_DIMS = (((0,), (0,)), ((), ()))


def _cparams(sem):
    return pltpu.CompilerParams(dimension_semantics=sem, vmem_limit_bytes=VMEM_LIMIT)


def _norm_proj_kernel(x_ref, g_ref, w_ref, cs_ref, o_ref, c0_ref, c1_ref, c2_ref, h_ref, ys_ref,
                      *, n_gate_tiles, n_main_tiles, dils):
    j = pl.program_id(1)
    tm = x_ref.shape[0]

    @pl.when(j == 0)
    def _():
        x = x_ref[...]
        ms = jnp.mean(x * x, axis=-1, keepdims=True)
        h_ref[...] = (x * lax.rsqrt(ms + EPS) * g_ref[...]).astype(BF16)

    def project():
        return jnp.dot(h_ref[...], w_ref[...], preferred_element_type=F32) * cs_ref[...]

    @pl.when(j < n_gate_tiles)
    def _():
        o_ref[...] = jax.nn.sigmoid(project()).astype(BF16)

    @pl.when((j >= n_gate_tiles) & (j < n_main_tiles))
    def _():
        o_ref[...] = project().astype(BF16)

    for g, (c_ref, dil) in enumerate(zip((c0_ref, c1_ref, c2_ref), dils)):
        @pl.when(j == n_main_tiles + g)
        def _(c_ref=c_ref, dil=dil):
            y = project()
            if dil == 1:
                c_ref[0, 0] = y.astype(BF16)
            else:
                for c in range(y.shape[1] // LANES):
                    ys_ref[c] = y[:, c * LANES:(c + 1) * LANES]
                for r in range(dil):
                    for c in range(y.shape[1] // LANES):
                        c_ref[0, r, :, c * LANES:(c + 1) * LANES] = ys_ref[
                            c, pl.ds(r, tm // dil, stride=dil), :].astype(BF16)


def _norm_proj(x, gain, w, colscale, bsz, seq, *, tm=1024, tn=C_SLAB):
    n, d = x.shape
    tm = min(tm, seq)
    dils = tuple(dil for _, dil in C_GROUPS)
    assert seq % tm == 0 and tn == C_SLAB and MAIN_COLS % tn == 0 and GATE_COLS % tn == 0
    assert all(tm % (2 * SUBLANES * dil) == 0 for dil in dils) and len(dils) == 3
    n_main = MAIN_COLS // tn
    tps = seq // tm

    def c_spec(dil):
        return pl.BlockSpec((1, dil, tm // dil, tn), lambda i, j: (i // tps, 0, i % tps, 0))

    return pl.pallas_call(
        functools.partial(_norm_proj_kernel, n_gate_tiles=GATE_COLS // tn, n_main_tiles=n_main, dils=dils),
        grid=(n // tm, n_main + len(dils)),
        in_specs=[
            pl.BlockSpec((tm, d), lambda i, j: (i, 0)),
            pl.BlockSpec((1, d), lambda i, j: (0, 0)),
            pl.BlockSpec((d, tn), lambda i, j: (0, j)),
            pl.BlockSpec((1, tn), lambda i, j: (0, j)),
        ],
        out_specs=[pl.BlockSpec((tm, tn), lambda i, j: (i, jnp.minimum(j, n_main - 1)))]
        + [c_spec(dil) for dil in dils],
        out_shape=[jax.ShapeDtypeStruct((n, MAIN_COLS), BF16)]
        + [jax.ShapeDtypeStruct((bsz, dil, seq // dil, tn), BF16) for dil in dils],
        scratch_shapes=[pltpu.VMEM((tm, d), BF16), pltpu.VMEM((tn // LANES, tm, LANES), F32)],
        compiler_params=_cparams(("parallel", "arbitrary")),
        name="norm_proj",
    )(x, gain.reshape(1, d), w, colscale)


def _prep_a_kernel(y_ref, cos_ref, sin_ref, qg_ref, kg_ref, bd_ref, q_out, k_out, v_out):
    tm = y_ref.shape[0]
    lane = lax.broadcasted_iota(jnp.int32, (tm, LANES), 1)
    first16 = (lane % 32) < 16
    half0 = lane < HEAD_DIM
    cos = cos_ref[...]
    sin = sin_ref[...]
    bd = bd_ref[...]

    def norm_rope(xb, gain, out_scale):
        x = xb.astype(F32)
        ss = jnp.dot(x * x, bd, preferred_element_type=F32, precision=lax.Precision.HIGHEST)
        xn = x * lax.rsqrt(ss * (1.0 / HEAD_DIM) + EPS) * gain
        sw = jnp.where(first16, pltpu.roll(xn, LANES - 16, 1), pltpu.roll(xn, 16, 1))
        return (xn * cos + sw * sin) * out_scale

    qg = qg_ref[...]
    for c in range(A_Q // LANES):
        q_out[:, c * LANES:(c + 1) * LANES] = norm_rope(
            y_ref[:, c * LANES:(c + 1) * LANES], qg, QK_SCALE).astype(BF16)

    k = norm_rope(y_ref[:, A_Q:A_Q + LANES], kg_ref[...], 1.0)
    kr = pltpu.roll(k, HEAD_DIM, 1)
    k_out[:, 0:LANES] = jnp.where(half0, k, kr).astype(BF16)
    k_out[:, LANES:2 * LANES] = jnp.where(half0, kr, k).astype(BF16)

    v = y_ref[:, A_Q + LANES:A_Q + 2 * LANES].astype(F32)
    vr = pltpu.roll(v, HEAD_DIM, 1)
    v_out[:, 0:LANES] = jnp.where(half0, v, vr).astype(BF16)
    v_out[:, LANES:2 * LANES] = jnp.where(half0, vr, v).astype(BF16)


def _prep_a(y, cos_t, sin_t, qg, kg, bd, seq, *, tm=512):
    n = y.shape[0]
    assert OFF_AQ % A_SLAB == 0 and seq % tm == 0 and A_KV == LANES
    spt = seq // tm
    return pl.pallas_call(
        _prep_a_kernel,
        grid=(n // tm,),
        in_specs=[
            pl.BlockSpec((tm, A_SLAB), lambda i: (i, OFF_AQ // A_SLAB)),
            pl.BlockSpec((tm, LANES), lambda i: (i % spt, 0)),
            pl.BlockSpec((tm, LANES), lambda i: (i % spt, 0)),
            pl.BlockSpec((1, LANES), lambda i: (0, 0)),
            pl.BlockSpec((1, LANES), lambda i: (0, 0)),
            pl.BlockSpec((LANES, LANES), lambda i: (0, 0)),
        ],
        out_specs=[
            pl.BlockSpec((tm, A_Q), lambda i: (i, 0)),
            pl.BlockSpec((tm, 2 * LANES), lambda i: (i, 0)),
            pl.BlockSpec((tm, 2 * LANES), lambda i: (i, 0)),
        ],
        out_shape=[
            jax.ShapeDtypeStruct((n, A_Q), BF16),
            jax.ShapeDtypeStruct((n, 2 * LANES), BF16),
            jax.ShapeDtypeStruct((n, 2 * LANES), BF16),
        ],
        compiler_params=_cparams(("parallel",)),
        name="prep_a",
    )(y, cos_t, sin_t, qg, kg, bd)


def _colmax(s):
    tk, r = s.shape
    m8 = jnp.max(s.reshape(tk // SUBLANES, SUBLANES, r), axis=0)
    return jnp.max(m8, axis=0, keepdims=True)


def _softmax_pv(s, mb, v_ref, m_ref, l_ref, acc_ref):
    tk, r = s.shape
    m_prev = m_ref[...]
    m_cur = jnp.maximum(m_prev, mb)
    alpha = jnp.exp2(m_prev - m_cur)
    p = jnp.exp2(s - m_cur)
    l_ref[...] = alpha * l_ref[...] + jnp.sum(p.reshape(tk // SUBLANES, SUBLANES, r), axis=0)
    acc_ref[...] = alpha * acc_ref[...] + lax.dot_general(
        v_ref[...], p.astype(BF16), TN_DIMS, preferred_element_type=F32)
    m_ref[...] = m_cur


def _flash_steps(t, nk, scores_fn, init_fn, finish_fn, v_ref, s_ref, mb_ref, m_ref, l_ref, acc_ref):
    def produce(cur):
        for col, s_new in scores_fn():
            width = s_new.shape[1]
            mb_ref[cur, :, col:col + width] = _colmax(s_new)
            s_ref[cur, :, col:col + width] = s_new

    def consume(prev):
        _softmax_pv(s_ref[prev], mb_ref[prev], v_ref, m_ref, l_ref, acc_ref)

    @pl.when(t == 0)
    def _():
        m_ref[...] = jnp.full(m_ref.shape, NEG_BIG, F32)
        l_ref[...] = jnp.zeros(l_ref.shape, F32)
        acc_ref[...] = jnp.zeros(acc_ref.shape, F32)
        init_fn()
        produce(0)

    for par in (0, 1):
        @pl.when((t > 0) & (t < nk) & (t % 2 == par))
        def _(par=par):
            produce(par)
            consume(1 - par)

    @pl.when(t == nk)
    def _():
        consume((nk - 1) % 2)
        finish_fn()


def _flash_scratch(tk, r):
    return [
        pltpu.VMEM((r, LANES), BF16),
        pltpu.VMEM((2, tk, r), F32),
        pltpu.VMEM((2, 1, r), F32),
        pltpu.VMEM((1, r), F32),
        pltpu.VMEM((SUBLANES, r), F32),
        pltpu.VMEM((LANES, r), F32),
    ]


def _flash_a_kernel(q_ref, k_ref, v_ref, o_ref, qs_ref, s_ref, mb_ref, m_ref, l_ref, acc_ref, *, tq, nk):
    t = pl.program_id(3)
    grp = A_HEADS // A_KV_HEADS
    lane = lax.broadcasted_iota(jnp.int32, (tq, LANES), 1)
    half0 = lane < HEAD_DIM

    def init():
        for h in range(grp):
            c = h // 2
            x = q_ref[:, c * LANES:(c + 1) * LANES]
            keep = half0 if h % 2 == 0 else jnp.logical_not(half0)
            qs_ref[h * tq:(h + 1) * tq, :] = jnp.where(keep, x, jnp.zeros_like(x))

    def scores():
        yield 0, lax.dot_general(k_ref[...], qs_ref[...], NT_DIMS, preferred_element_type=F32)

    def finish():
        o_t = acc_ref[...] / jnp.sum(l_ref[...], axis=0, keepdims=True)
        for c in range(grp // 2):
            o0 = o_t[:, (2 * c) * tq:(2 * c + 1) * tq].T
            o1 = o_t[:, (2 * c + 1) * tq:(2 * c + 2) * tq].T
            o_ref[:, c * LANES:(c + 1) * LANES] = jnp.where(half0, o0, o1).astype(BF16)

    _flash_steps(t, nk, scores, init, finish, v_ref, s_ref, mb_ref, m_ref, l_ref, acc_ref)


def _flash_a(qn, kd, vd, bsz, seq, *, tq=1024, tk=1024):
    n = qn.shape[0]
    tk = min(tk, seq)
    assert seq % tq == 0 and seq % tk == 0
    nq, nk = seq // tq, seq // tk
    grp = A_HEADS // A_KV_HEADS
    qw = grp * HEAD_DIM
    return pl.pallas_call(
        functools.partial(_flash_a_kernel, tq=tq, nk=nk),
        grid=(bsz, A_KV_HEADS, nq, nk + 1),
        in_specs=[
            pl.BlockSpec((tq, qw), lambda b, kv, qi, t: (b * nq + qi, kv)),
            pl.BlockSpec((tk, LANES), lambda b, kv, qi, t: (b * nk + jnp.minimum(t, nk - 1), kv)),
            pl.BlockSpec((tk, LANES), lambda b, kv, qi, t: (b * nk + jnp.maximum(t - 1, 0), kv)),
        ],
        out_specs=pl.BlockSpec((tq, qw), lambda b, kv, qi, t: (b * nq + qi, kv)),
        out_shape=jax.ShapeDtypeStruct((n, A_Q), BF16),
        scratch_shapes=_flash_scratch(tk, grp * tq),
        compiler_params=_cparams(("parallel", "parallel", "parallel", "arbitrary")),
        name="flash_a",
    )(qn, kd, vd)


def _flash_b_kernel(lam_ref, q_ref, k_ref, v_ref, bias_ref, sub_ref, o_ref,
                    qs_ref, s_ref, mb_ref, m_ref, l_ref, acc_ref, *, tq, nk, out_scale):
    qi = pl.program_id(2)
    t = pl.program_id(3)
    lane = lax.broadcasted_iota(jnp.int32, (tq, LANES), 1)
    half0 = lane < HEAD_DIM

    def init():
        x = q_ref[...]
        zero = jnp.zeros_like(x)
        qs_ref[0:tq, :] = jnp.where(half0, x, zero)
        qs_ref[tq:2 * tq, :] = jnp.where(half0, zero, x)

    def scores():
        bias = bias_ref[0, jnp.clip(t - qi, -2, 2) + 2]
        k = k_ref[...]
        for mp in range(2):
            yield mp * tq, lax.dot_general(
                k, qs_ref[mp * tq:(mp + 1) * tq, :], NT_DIMS, preferred_element_type=F32) + bias

    def finish():
        lam = lam_ref[0]
        o_t = acc_ref[...] / jnp.sum(l_ref[...], axis=0, keepdims=True)
        o = (o_t[:, 0:tq] - lam * o_t[:, tq:2 * tq]).T
        ms = jnp.mean(o * o, axis=-1, keepdims=True)
        o_ref[...] = (o * lax.rsqrt(ms + EPS) * sub_ref[...] * out_scale).astype(BF16)

    _flash_steps(t, nk, scores, init, finish, v_ref, s_ref, mb_ref, m_ref, l_ref, acc_ref)


def _flash_b(y, bias_tiles, lam, subln, bsz, seq, lambda_init, *, t):
    n = y.shape[0]
    assert seq % t == 0 and bias_tiles.shape == (B_HEADS, 5, t, t)
    nq = seq // t
    qb, kb, vb = OFF_BQ // LANES, OFF_BK // LANES, OFF_BV // LANES
    return pl.pallas_call(
        functools.partial(_flash_b_kernel, tq=t, nk=nq, out_scale=1.0 - lambda_init),
        grid=(B_HEADS, bsz, nq, nq + 1),
        in_specs=[
            pl.BlockSpec(memory_space=pltpu.SMEM),
            pl.BlockSpec((t, LANES), lambda h, b, qi, ts: (b * nq + qi, qb + h)),
            pl.BlockSpec((t, LANES), lambda h, b, qi, ts: (b * nq + jnp.minimum(ts, nq - 1), kb + h)),
            pl.BlockSpec((t, LANES), lambda h, b, qi, ts: (b * nq + jnp.maximum(ts - 1, 0), vb + h)),
            pl.BlockSpec((1, 5, t, t), lambda h, b, qi, ts: (h, 0, 0, 0), pipeline_mode=pl.Buffered(1)),
            pl.BlockSpec((1, LANES), lambda h, b, qi, ts: (0, 0)),
        ],
        out_specs=pl.BlockSpec((t, LANES), lambda h, b, qi, ts: (b * nq + qi, h)),
        out_shape=jax.ShapeDtypeStruct((n, B_V), BF16),
        scratch_shapes=_flash_scratch(t, 2 * t),
        compiler_params=_cparams(("parallel", "parallel", "parallel", "arbitrary")),
        name="flash_b",
    )(lam, y, y, y, bias_tiles, subln)


def _window_kernel(q_ref, kp_ref, km_ref, kn_ref, vp_ref, vm_ref, vn_ref, bias_ref,
                   o_ref, lse_ref, *, tq, halo, length, dil):
    i = pl.program_id(1)
    nkeys = tq + 2 * halo
    ukey = i * tq - halo + lax.broadcasted_iota(jnp.int32, (nkeys, 2 * tq), 0)
    key_ok = (ukey >= 0) & (ukey < length)
    lane = lax.broadcasted_iota(jnp.int32, (tq, LANES), 1)
    half0 = lane < HEAD_DIM
    top_half = lax.broadcasted_iota(jnp.int32, (LANES, tq), 0) < HEAD_DIM

    def one_residue(r, rows, edge):
        q = q_ref[0, r]
        kcat = jnp.concatenate([kp_ref[0, r], km_ref[0, r], kn_ref[0, r]], axis=0)
        vcat = jnp.concatenate([vp_ref[0, r], vm_ref[0, r], vn_ref[0, r]], axis=0)
        for c in range(C_HEADS // 2):
            qc = q[:, c * LANES:(c + 1) * LANES]
            kc = kcat[:, c * LANES:(c + 1) * LANES]
            vc = vcat[:, c * LANES:(c + 1) * LANES]
            zero = jnp.zeros_like(qc)
            qpair = jnp.concatenate([jnp.where(half0, qc, zero), jnp.where(half0, zero, qc)], axis=0)
            s = lax.dot_general(kc, qpair, NT_DIMS, preferred_element_type=F32)
            s = s + bias_ref[c]
            if edge:
                s = jnp.where(key_ok, s, NEG_BIG)
            m = _colmax(s)
            e = jnp.exp2(s - m)
            den = jnp.sum(e.reshape(nkeys // SUBLANES, SUBLANES, 2 * tq), axis=0)
            den = jnp.sum(den, axis=0, keepdims=True)
            o_t = lax.dot_general(vc, e.astype(BF16), TN_DIMS, preferred_element_type=F32) / den
            lse = (m + jnp.log2(den)) * LN2
            o_pair = jnp.where(top_half, o_t[:, 0:tq], o_t[:, tq:2 * tq])
            lse_pair = jnp.where(top_half, lse[:, 0:tq], lse[:, tq:2 * tq])
            o_ref[c, rows, :] = o_pair.T
            lse_ref[c, rows, :] = lse_pair.T

    def all_residues(edge):
        if dil == 1:
            one_residue(0, pl.ds(0, tq), edge)
        else:
            def body(r, carry):
                one_residue(r, pl.ds(r, tq, stride=dil), edge)
                return carry
            lax.fori_loop(0, dil, body, 0, unroll=2)

    on_edge = (i == 0) | (i == pl.num_programs(1) - 1)

    @pl.when(on_edge)
    def _():
        all_residues(True)

    @pl.when(jnp.logical_not(on_edge))
    def _():
        all_residues(False)


def _window_group(cg, bias_tile, dil, bsz, seq, *, tq, halo):
    length = seq // dil
    tq = min(tq, length)
    cw = C_OUT
    assert length % tq == 0 and tq % halo == 0 and length % halo == 0
    assert bias_tile.shape == (C_HEADS // 2, tq + 2 * halo, 2 * tq)
    nt = length // tq
    hb = tq // halo
    nhb = length // halo
    main = lambda col: pl.BlockSpec((1, dil, tq, cw), lambda b, i: (b, 0, i, col))
    prev = lambda col: pl.BlockSpec(
        (1, dil, halo, cw), lambda b, i: (b, 0, jnp.maximum(i * hb - 1, 0), col))
    nxt = lambda col: pl.BlockSpec(
        (1, dil, halo, cw), lambda b, i: (b, 0, jnp.minimum((i + 1) * hb, nhb - 1), col))
    out = pl.BlockSpec((cw // LANES, tq * dil, LANES), lambda b, i: (0, b * nt + i, 0))
    return pl.pallas_call(
        functools.partial(_window_kernel, tq=tq, halo=halo, length=length, dil=dil),
        grid=(bsz, nt),
        in_specs=[main(0), prev(1), main(1), nxt(1), prev(2), main(2), nxt(2),
                  pl.BlockSpec(bias_tile.shape, lambda b, i: (0, 0, 0))],
        out_specs=[out, out],
        out_shape=[jax.ShapeDtypeStruct((cw // LANES, bsz * seq, LANES), F32)] * 2,
        compiler_params=_cparams(("parallel", "parallel")),
        name=f"window_d{dil}",
    )(cg, cg, cg, cg, cg, cg, cg, bias_tile)


def _merge_kernel(x_ref, oa_ref, ob_ref, oc0_ref, oc1_ref, oc2_ref, ls0_ref, ls1_ref, ls2_ref,
                  g0_ref, g1_ref, g2_ref, wb_ref, wo_ref, o_ref):
    planes = lambda ref: jnp.concatenate([ref[c] for c in range(C_OUT // LANES)], axis=1)
    ls = [planes(ls0_ref), planes(ls1_ref), planes(ls2_ref)]
    ocs = [planes(oc0_ref), planes(oc1_ref), planes(oc2_ref)]
    m = jnp.maximum(jnp.maximum(ls[0], ls[1]), ls[2])
    es = [jnp.exp(l - m) for l in ls]
    den = es[0] + es[1] + es[2]
    oc = (es[0] * ocs[0] + es[1] * ocs[1] + es[2] * ocs[2]) / den
    ya = jnp.dot(oa_ref[...], wb_ref[0:A_Q, :], preferred_element_type=F32)
    yb = jnp.dot(ob_ref[...], wb_ref[A_Q:A_Q + B_V, :], preferred_element_type=F32)
    yc = jnp.dot(oc.astype(BF16), wb_ref[A_Q + B_V:, :], preferred_element_type=F32)
    merged = (g0_ref[...].astype(F32) * ya + g1_ref[...].astype(F32) * yb
              + g2_ref[...].astype(F32) * yc)
    o_ref[...] = x_ref[...] + jnp.dot(merged.astype(BF16), wo_ref[...], preferred_element_type=F32)


def _merge(x, oa, ob, ocs, lss, y, wb, wo, *, tm=512):
    n, d = x.shape
    assert n % tm == 0 and OFF_GATE == 0
    row = lambda w: pl.BlockSpec((tm, w), lambda i: (i, 0))
    gate = lambda j: pl.BlockSpec((tm, d), lambda i: (i, j))
    plane = pl.BlockSpec((C_OUT // LANES, tm, LANES), lambda i: (0, i, 0))
    return pl.pallas_call(
        _merge_kernel,
        grid=(n // tm,),
        in_specs=[row(d), row(A_Q), row(B_V)] + [plane] * 6 + [gate(0), gate(1), gate(2)] + [
            pl.BlockSpec(wb.shape, lambda i: (0, 0)),
            pl.BlockSpec(wo.shape, lambda i: (0, 0)),
        ],
        out_specs=row(d),
        out_shape=jax.ShapeDtypeStruct((n, d), F32),
        compiler_params=_cparams(("parallel",)),
        name="merge",
    )(x, oa, ob, *ocs, *lss, y, y, y, wb, wo)


def _rms(x, gain):
    ms = jnp.mean(x * x, axis=-1, keepdims=True)
    return x * lax.rsqrt(ms + EPS) * gain


def _finish(x_ref, acc_ref, o_ref, fg_ref):
    out = x_ref[...] + acc_ref[...]
    if fg_ref is not None:
        out = _rms(out, fg_ref[...])
    o_ref[...] = out


def _ffn_kernel(x_ref, g_ref, w1_ref, w3_ref, w2_ref, *rest, nf, final_norm):
    if final_norm:
        fg_ref, o_ref, h_ref, acc_ref = rest
    else:
        fg_ref = None
        o_ref, h_ref, acc_ref = rest
    j = pl.program_id(1)

    @pl.when(j == 0)
    def _():
        h_ref[...] = _rms(x_ref[...], g_ref[...]).astype(BF16)
        acc_ref[...] = jnp.zeros(acc_ref.shape, F32)

    h = h_ref[...]
    a = jnp.dot(h, w1_ref[...], preferred_element_type=F32)
    b = jnp.dot(h, w3_ref[...], preferred_element_type=F32)
    t = (jax.nn.silu(a) * b).astype(BF16)
    acc_ref[...] += jnp.dot(t, w2_ref[...], preferred_element_type=F32)

    @pl.when(j == nf - 1)
    def _():
        _finish(x_ref, acc_ref, o_ref, fg_ref)


def _ffn(x, gain, w1, w3, w2, final_gain=None, *, tm=512, tf=1408):
    n, d = x.shape
    ff = w1.shape[1]
    assert n % tm == 0 and ff % tf == 0
    nf = ff // tf
    final_norm = final_gain is not None
    in_specs = [
        pl.BlockSpec((tm, d), lambda i, j: (i, 0)),
        pl.BlockSpec((1, d), lambda i, j: (0, 0)),
        pl.BlockSpec((d, tf), lambda i, j: (0, j)),
        pl.BlockSpec((d, tf), lambda i, j: (0, j)),
        pl.BlockSpec((tf, d), lambda i, j: (j, 0)),
    ]
    args = [x, gain.reshape(1, d), w1, w3, w2]
    if final_norm:
        in_specs.append(pl.BlockSpec((1, d), lambda i, j: (0, 0)))
        args.append(final_gain.reshape(1, d))
    return pl.pallas_call(
        functools.partial(_ffn_kernel, nf=nf, final_norm=final_norm),
        grid=(n // tm, nf),
        in_specs=in_specs,
        out_specs=pl.BlockSpec((tm, d), lambda i, j: (i, 0)),
        out_shape=jax.ShapeDtypeStruct((n, d), F32),
        scratch_shapes=[pltpu.VMEM((tm, d), BF16), pltpu.VMEM((tm, d), F32)],
        compiler_params=_cparams(("parallel", "arbitrary")),
        name="ffn",
    )(*args)


def _top2_gate(logits):
    lane = lax.broadcasted_iota(jnp.int32, logits.shape, 1)
    m1 = jnp.max(logits, axis=-1, keepdims=True)
    i1 = jnp.min(jnp.where(logits == m1, lane, LANES), axis=-1, keepdims=True)
    rest = jnp.where(lane == i1, -jnp.inf, logits)
    m2 = jnp.max(rest, axis=-1, keepdims=True)
    i2 = jnp.min(jnp.where(rest == m2, lane, LANES), axis=-1, keepdims=True)
    e2 = jnp.exp(m2 - m1)
    w1 = 1.0 / (1.0 + e2)
    w2 = e2 / (1.0 + e2)
    return jnp.where(lane == i1, w1, 0.0) + jnp.where(lane == i2, w2, 0.0)


def _moe_route_kernel(x_ref, g_ref, r_ref, tri_ref, cum_ref, h_ref, gate_ref, rank_ref, mask_ref, cnt_ref):
    hf = _rms(x_ref[...], g_ref[...])
    h_ref[...] = hf.astype(BF16)
    logits = jnp.dot(hf, r_ref[...], preferred_element_type=F32, precision=lax.Precision.HIGHEST)
    lane = lax.broadcasted_iota(jnp.int32, logits.shape, 1)
    gate = _top2_gate(jnp.where(lane < N_EXPERTS, logits, -jnp.inf))
    gate_ref[...] = gate
    sel = jnp.where(gate.T[0:N_EXPERTS, :] != 0.0, 1.0, 0.0)
    mask_ref[0] = sel
    selb = sel.astype(BF16)
    rank_ref[0] = jnp.dot(selb, tri_ref[...], preferred_element_type=F32)
    cnt_ref[0] = jnp.dot(selb, cum_ref[...], preferred_element_type=F32).astype(jnp.int32)


def _moe_route(x, gain, router_p, tri, cum, *, tm):
    n, d = x.shape
    nt = n // tm
    full = lambda shape: pl.BlockSpec(shape, lambda i: (0,) * len(shape))
    per_tile = lambda shape: pl.BlockSpec((1,) + shape, lambda i: (i, 0, 0))
    return pl.pallas_call(
        _moe_route_kernel,
        grid=(nt,),
        in_specs=[pl.BlockSpec((tm, d), lambda i: (i, 0)), full((1, d)), full((d, LANES)),
                  full((tm, tm)), full((tm, LANES))],
        out_specs=[pl.BlockSpec((tm, d), lambda i: (i, 0)), pl.BlockSpec((tm, LANES), lambda i: (i, 0)),
                   per_tile((N_EXPERTS, tm)), per_tile((N_EXPERTS, tm)), per_tile((N_EXPERTS, LANES))],
        out_shape=[jax.ShapeDtypeStruct((n, d), BF16), jax.ShapeDtypeStruct((n, LANES), F32),
                   jax.ShapeDtypeStruct((nt, N_EXPERTS, tm), F32),
                   jax.ShapeDtypeStruct((nt, N_EXPERTS, tm), F32),
                   jax.ShapeDtypeStruct((nt, N_EXPERTS, LANES), jnp.int32)],
        compiler_params=_cparams(("parallel",)),
        name="moe_route",
    )(x, gain.reshape(1, d), router_p, tri, cum)


def _moe_experts_kernel(cnt_ref, h_ref, gate_ref, rank_ref, mask_ref, w1_ref, w3_ref, w2_ref, o_ref,
                        hc_ref, oc_ref, *, nf, nb, ch, wb, fb):
    i, e, j = pl.program_id(0), pl.program_id(1), pl.program_id(2)
    tm, d = h_ref.shape
    n_rows = cnt_ref[i, e, nb]
    n_chunks = (n_rows + ch - 1) // ch

    def onehot(c, b):
        rank = rank_ref[0, pl.ds(e, 1), b * wb:(b + 1) * wb]
        sel = mask_ref[0, pl.ds(e, 1), b * wb:(b + 1) * wb]
        srow = (c * ch + lax.broadcasted_iota(jnp.int32, (ch, 1), 0)).astype(F32)
        return jnp.where((rank == srow) & (sel > 0.5), 1.0, 0.0).astype(BF16)

    def hits(c, b):
        return (cnt_ref[i, e, b] < (c + 1) * ch) & (cnt_ref[i, e, b + 1] > c * ch)

    @pl.when((e == 0) & (j == 0))
    def _():
        o_ref[...] = jnp.zeros(o_ref.shape, F32)

    @pl.when(j == 0)
    def _():
        def gather(c, carry):
            rows = pl.ds(pl.multiple_of(c * ch, ch), ch)
            hc_ref[rows, :] = jnp.zeros((ch, d), BF16)
            for b in range(nb):
                @pl.when(hits(c, b))
                def _(b=b):
                    picked = jnp.dot(onehot(c, b), h_ref[b * wb:(b + 1) * wb, :], preferred_element_type=F32)
                    hc_ref[rows, :] += picked.astype(BF16)
            return carry
        lax.fori_loop(0, n_chunks, gather, 0)

    def expert(row0, size):
        rows = pl.ds(pl.multiple_of(row0, size), size)
        hcb = hc_ref[rows, :]
        a = jnp.dot(hcb, w1_ref[0], preferred_element_type=F32)
        b3 = jnp.dot(hcb, w3_ref[0], preferred_element_type=F32)
        r = jnp.dot((jax.nn.silu(a) * b3).astype(BF16), w2_ref[0], preferred_element_type=F32)

        @pl.when(j == 0)
        def _():
            oc_ref[rows, :] = r

        @pl.when(j > 0)
        def _():
            oc_ref[rows, :] += r

    def expert_body(f, carry):
        expert(f * fb, fb)
        return carry
    n_full = (n_chunks * ch) // fb
    lax.fori_loop(0, n_full, expert_body, 0)

    @pl.when(n_chunks * ch > n_full * fb)
    def _():
        expert(n_full * fb, ch)

    @pl.when(j == nf - 1)
    def _():
        def scatter(c, carry):
            ocb = oc_ref[pl.ds(pl.multiple_of(c * ch, ch), ch), :].astype(BF16)
            for b in range(nb):
                @pl.when(hits(c, b))
                def _(b=b):
                    back = lax.dot_general(onehot(c, b), ocb, TN_DIMS, preferred_element_type=F32)
                    g = gate_ref[b * wb:(b + 1) * wb, :]
                    lane = lax.broadcasted_iota(jnp.int32, g.shape, 1)
                    gcol = jnp.sum(jnp.where(lane == e, g, 0.0), axis=-1, keepdims=True)
                    o_ref[b * wb:(b + 1) * wb, :] += gcol * back
            return carry
        lax.fori_loop(0, n_chunks, scatter, 0)


def _moe_experts(cnt, h, gate, rank, mask, w1, w3, w2, *, tm, tf=1408, ch=128, wb=256, fb=256):
    n, d = h.shape
    ne, _, ff = w1.shape
    assert n % tm == 0 and ff % tf == 0 and tm % wb == 0 and tm % fb == 0 and fb == 2 * ch
    nf, nb = ff // tf, tm // wb
    assert cnt.shape == (n // tm, ne, nb + 1)
    grid_spec = pltpu.PrefetchScalarGridSpec(
        num_scalar_prefetch=1,
        grid=(n // tm, ne, nf),
        in_specs=[
            pl.BlockSpec((tm, d), lambda i, e, j, c: (i, 0), pipeline_mode=pl.Buffered(1)),
            pl.BlockSpec((tm, LANES), lambda i, e, j, c: (i, 0)),
            pl.BlockSpec((1, ne, tm), lambda i, e, j, c: (i, 0, 0)),
            pl.BlockSpec((1, ne, tm), lambda i, e, j, c: (i, 0, 0)),
            pl.BlockSpec((1, d, tf), lambda i, e, j, c: (e, 0, j)),
            pl.BlockSpec((1, d, tf), lambda i, e, j, c: (e, 0, j)),
            pl.BlockSpec((1, tf, d), lambda i, e, j, c: (e, j, 0)),
        ],
        out_specs=pl.BlockSpec((tm, d), lambda i, e, j, c: (i, 0), pipeline_mode=pl.Buffered(1)),
        scratch_shapes=[pltpu.VMEM((tm, d), BF16), pltpu.VMEM((tm, d), F32)],
    )
    return pl.pallas_call(
        functools.partial(_moe_experts_kernel, nf=nf, nb=nb, ch=ch, wb=wb, fb=fb),
        grid_spec=grid_spec,
        out_shape=jax.ShapeDtypeStruct((n, d), F32),
        compiler_params=_cparams(("parallel", "arbitrary", "arbitrary")),
        name="moe_experts",
    )(cnt, h, gate, rank, mask, w1, w3, w2)


def _residual_kernel(x_ref, y_ref, *rest, final_norm):
    if final_norm:
        fg_ref, o_ref = rest
    else:
        fg_ref = None
        (o_ref,) = rest
    _finish(x_ref, y_ref, o_ref, fg_ref)


def _residual(x, y, final_gain=None, *, tm=1024):
    n, d = x.shape
    assert n % tm == 0
    final_norm = final_gain is not None
    row = pl.BlockSpec((tm, d), lambda i: (i, 0))
    in_specs, args = [row, row], [x, y]
    if final_norm:
        in_specs.append(pl.BlockSpec((1, d), lambda i: (0, 0)))
        args.append(final_gain.reshape(1, d))
    return pl.pallas_call(
        functools.partial(_residual_kernel, final_norm=final_norm),
        grid=(n // tm,),
        in_specs=in_specs,
        out_specs=row,
        out_shape=jax.ShapeDtypeStruct((n, d), F32),
        compiler_params=_cparams(("parallel",)),
        name="residual",
    )(*args)


def _moe(x, gain, router_p, w1, w3, w2, final_gain=None, *, tm=2048, wb=256):
    n = x.shape[0]
    tm = min(tm, n)
    t = jnp.arange(tm)
    tri = (t[:, None] < t[None, :]).astype(BF16)
    cum = (t[:, None] < jnp.arange(LANES)[None, :] * wb).astype(BF16)
    h, gate, rank, mask, cnt = _moe_route(x, gain, router_p, tri, cum, tm=tm)
    y = _moe_experts(cnt[:, :, :tm // wb + 1], h, gate, rank, mask, w1, w3, w2, tm=tm, wb=wb)
    return _residual(x, y, final_gain)


def _t5_bucket(rel):
    nb = NUM_BUCKETS // 2
    max_exact = nb // 2
    ret = jnp.where(rel > 0, nb, 0)
    n = jnp.abs(rel)
    large = max_exact + (jnp.log(jnp.maximum(n, max_exact).astype(F32) / max_exact)
                         / math.log(MAX_DISTANCE / max_exact) * (nb - max_exact)).astype(jnp.int32)
    large = jnp.minimum(large, nb - 1)
    return ret + jnp.where(n < max_exact, n, large)


def _toeplitz_kernel(w_ref, o_ref):
    rows, cols = o_ref.shape[1:]
    w = jnp.broadcast_to(w_ref[0], (rows, w_ref.shape[2]))
    o_ref[0] = pltpu.roll(w, 0, 1, stride=1, stride_axis=0)[:, :cols]


def _toeplitz(value_of_rel, rows, cols, col0):
    period = -(-(rows + cols) // LANES) * LANES
    k = np.arange(period)
    rel = jnp.asarray(col0 + np.where(k < cols, k, k - period), jnp.int32)
    w = value_of_rel(rel).astype(F32)
    lead = w.shape[:-1]
    n = int(np.prod(lead)) if lead else 1
    out = pl.pallas_call(
        _toeplitz_kernel,
        grid=(n,),
        in_specs=[pl.BlockSpec((1, 1, period), lambda i: (i, 0, 0))],
        out_specs=pl.BlockSpec((1, rows, cols), lambda i: (i, 0, 0)),
        out_shape=jax.ShapeDtypeStruct((n, rows, cols), F32),
        compiler_params=_cparams(("parallel",)),
        name="toeplitz",
    )(w.reshape(n, 1, period))
    return out.reshape(lead + (rows, cols))


def _rope_tables(seq):
    t = jnp.arange(seq)
    row = (t // GRID_W).astype(F32)
    col = (t % GRID_W).astype(F32)
    half = HEAD_DIM // 2
    inv = ROPE_THETA ** (-jnp.arange(0, half, 2, dtype=F32) / half)
    ar = row[:, None] * inv
    ac = col[:, None] * inv
    cos = jnp.concatenate([jnp.cos(ar), jnp.cos(ar), jnp.cos(ac), jnp.cos(ac)], axis=1)
    sin = jnp.concatenate([-jnp.sin(ar), jnp.sin(ar), -jnp.sin(ac), jnp.sin(ac)], axis=1)
    return jnp.tile(cos, (1, LANES // HEAD_DIM)), jnp.tile(sin, (1, LANES // HEAD_DIM))


def _bias_tiles_b(rel_bias, t):
    cols = rel_bias[:, :B_HEADS].astype(F32).T * LOG2E
    by_rel = lambda rel: cols[:, _t5_bucket(rel)]
    tiles = []
    for delta in (-2, -1, 0, 1, 2):
        if abs(delta) == 2:
            far = by_rel(jnp.full((1,), delta * t, jnp.int32))
            tiles.append(jnp.broadcast_to(far[:, :, None], (B_HEADS, t, t)))
        else:
            tiles.append(_toeplitz(lambda r: by_rel(-r), t, t, -delta * t))
    return jnp.stack(tiles, axis=1)


def _bias_tile_c(rel_bias, g, dil, span, tq, halo):
    cols = rel_bias[:, B_HEADS + g * C_HEADS:B_HEADS + (g + 1) * C_HEADS].astype(F32).T * LOG2E
    nkeys = tq + 2 * halo

    def by_rel(rel):
        return jnp.where((jnp.abs(rel) <= span)[None, :], cols[:, _t5_bucket(rel * dil)], NEG_BIG)

    per_head = _toeplitz(lambda r: by_rel(-r), nkeys, tq, halo)
    pairs = per_head.reshape(C_HEADS // 2, 2, nkeys, tq).transpose(0, 2, 1, 3)
    return pairs.reshape(C_HEADS // 2, nkeys, 2 * tq)


def _layout_w_in(w):
    a0 = 0
    b0 = a0 + A_SLAB
    c0 = b0 + 2 * B_QK + B_V
    g0 = c0 + 3 * C_QKV
    parts = [w[..., g0:g0 + GATE_COLS], w[..., a0:b0], w[..., b0:c0]]
    for g in range(N_CGROUPS):
        for kind in range(3):
            s = c0 + kind * C_QKV + g * C_OUT
            parts.append(w[..., s:s + C_OUT])
    return jnp.concatenate(parts, axis=-1)


def _trunk(x3, p):
    bsz, seq, d = x3.shape
    n = bsz * seq
    x = x3.reshape(n, d)
    depth = p["ln1"].shape[0]
    assert seq >= MAX_DISTANCE and depth > 0
    cos_t, sin_t = _rope_tables(seq)
    tb = min(1024, seq)
    bias_b = _bias_tiles_b(p["rel_bias"], tb)
    win_halo = 64
    win_tq = [min(512, seq // dil, 4096 // dil) for _, dil in C_GROUPS]
    bias_c = []
    for g, (window, dil) in enumerate(C_GROUPS):
        span = window // (2 * dil)
        assert span <= win_halo
        bias_c.append(_bias_tile_c(p["rel_bias"], g, dil, span, win_tq[g], win_halo))

    for l in range(depth):
        y, *cgs = _norm_proj(x, p["ln1"][l], p["w_in"][l], p["colscale"], bsz, seq)
        qn, kd, vd = _prep_a(y, cos_t, sin_t, p["qg"][l], p["kg"][l], p["bd"], seq)
        oa = _flash_a(qn, kd, vd, bsz, seq)
        lambda_init = 0.8 - 0.6 * math.exp(-0.3 * l)
        ob = _flash_b(y, bias_b, p["lam"][l], p["subln"][l], bsz, seq, lambda_init, t=tb)
        ocs, lss = [], []
        for g, (window, dil) in enumerate(C_GROUPS):
            oc, ls = _window_group(cgs[g], bias_c[g], dil, bsz, seq, tq=win_tq[g], halo=win_halo)
            ocs.append(oc)
            lss.append(ls)
        x = _merge(x, oa, ob, ocs, lss, y, p["w_br"][l], p["w_o"][l])
        fg = p["ln_f"] if l == depth - 1 else None
        jj = l // 2
        if l % 2 == 0:
            x = _ffn(x, p["ln2"][l], p["ffn_w1"][jj], p["ffn_w3"][jj], p["ffn_w2"][jj], fg)
        else:
            x = _moe(x, p["ln2"][l], p["router"][jj], p["exp_w1"][jj], p["exp_w3"][jj],
                     p["exp_w2"][jj], fg)
    return x.reshape(bsz, seq, d)


def kernel(x_prompt, x_sample, rel_bias, ln1, ln2, ln_f, w_in, a_qnorm, a_knorm, lam_q1, lam_k1,
           lam_q2, lam_k2, b_subln, w_br, w_o, ffn_w1, ffn_w3, ffn_w2, router, exp_w1, exp_w3, exp_w2):
    depth = ln1.shape[0]
    colscale = np.ones((1, IN_COLS), np.float32)
    colscale[:, OFF_BQ:OFF_BQ + B_QK] = QK_SCALE
    for g in range(N_CGROUPS):
        colscale[:, MAIN_COLS + g * C_SLAB:MAIN_COLS + g * C_SLAB + C_OUT] = QK_SCALE
    lam = (jnp.exp(jnp.sum(lam_q1.astype(F32) * lam_k1.astype(F32), axis=-1))
           - jnp.exp(jnp.sum(lam_q2.astype(F32) * lam_k2.astype(F32), axis=-1)))
    lam = lam + jnp.asarray([0.8 - 0.6 * math.exp(-0.3 * l) for l in range(depth)], F32)
    head_id = np.arange(LANES) // HEAD_DIM
    p = {
        "rel_bias": rel_bias,
        "ln1": ln1, "ln2": ln2, "ln_f": ln_f,
        "w_in": _layout_w_in(w_in).astype(BF16),
        "colscale": jnp.asarray(colscale),
        "qg": jnp.tile(a_qnorm.astype(F32), (1, LANES // HEAD_DIM)).reshape(depth, 1, LANES),
        "kg": jnp.tile(a_knorm.astype(F32), (1, LANES // HEAD_DIM)).reshape(depth, 1, LANES),
        "bd": jnp.asarray(head_id[:, None] == head_id[None, :], F32),
        "lam": lam.reshape(depth, 1),
        "subln": b_subln.astype(F32).reshape(depth, 1, 2 * HEAD_DIM),
        "w_br": w_br.astype(BF16), "w_o": w_o.astype(BF16),
        "ffn_w1": ffn_w1.astype(BF16), "ffn_w3": ffn_w3.astype(BF16), "ffn_w2": ffn_w2.astype(BF16),
        "router": jnp.pad(router.astype(F32), ((0, 0), (0, 0), (0, LANES - N_EXPERTS))),
        "exp_w1": exp_w1.astype(BF16), "exp_w3": exp_w3.astype(BF16), "exp_w2": exp_w2.astype(BF16),
    }
    return (_trunk(x_prompt, p), _trunk(x_sample, p))
```

```python
import functools
import math

import jax
import jax.numpy as jnp
import numpy as np
from jax import lax
from jax.experimental import pallas as pl
from jax.experimental.pallas import tpu as pltpu

F32 = jnp.float32
BF16 = jnp.bfloat16

D_MODEL = 1024
HEAD_DIM = 64
GRID_W = 64
ROPE_THETA = 10000.0
EPS = 1e-6
A_HEADS = 8
A_KV_HEADS = 2
B_HEADS = 4
C_GROUPS = ((128, 1), (512, 4), (2048, 16))
C_HEADS = 4
NUM_BUCKETS = 32
MAX_DISTANCE = 128
N_EXPERTS = 8
TOP_K = 2
N_BRANCH = 3

A_Q = A_HEADS * HEAD_DIM
A_KV = A_KV_HEADS * HEAD_DIM
B_QK = B_HEADS * 2 * HEAD_DIM
B_V = B_HEADS * 2 * HEAD_DIM
N_CGROUPS = len(C_GROUPS)
C_QKV = N_CGROUPS * C_HEADS * HEAD_DIM
C_OUT = C_HEADS * HEAD_DIM
GATE_COLS = N_BRANCH * D_MODEL
IN_COLS = A_Q + 2 * A_KV + 2 * B_QK + B_V + 3 * C_QKV + GATE_COLS

OFF_GATE = 0
OFF_AQ = GATE_COLS
A_SLAB = A_Q + 2 * A_KV
OFF_BQ = OFF_AQ + A_SLAB
OFF_BK = OFF_BQ + B_QK
OFF_BV = OFF_BK + B_QK
MAIN_COLS = OFF_BV + B_V
C_SLAB = 3 * C_OUT

LANES = 128
SUBLANES = 8
VMEM_LIMIT = 56 * 1024 * 1024
LOG2E = 1.4426950408889634
LN2 = 0.6931471805599453
QK_SCALE = HEAD_DIM ** -0.5 * LOG2E
NEG_BIG = -1e30

NT_DIMS = (((1,), (1,)), ((), ()))
TN_DIMS = (((0,), (0,)), ((), ()))


def _cparams(sem):
    return pltpu.CompilerParams(dimension_semantics=sem, vmem_limit_bytes=VMEM_LIMIT)


def _prep_a(y, cos, sin, qg, kg, bd, q_out, k_out, v_out):
    tm = y.shape[0]
    lane = lax.broadcasted_iota(jnp.int32, (tm, LANES), 1)
    first16 = (lane % 32) < 16
    half0 = lane < HEAD_DIM

    def norm_rope(x, gain, out_scale):
        ss = jnp.dot(x * x, bd, preferred_element_type=F32, precision=lax.Precision.HIGHEST)
        xn = x * lax.rsqrt(ss * (1.0 / HEAD_DIM) + EPS) * gain
        sw = jnp.where(first16, pltpu.roll(xn, LANES - 16, 1), pltpu.roll(xn, 16, 1))
        return (xn * cos + sw * sin) * out_scale

    for c in range(A_Q // LANES):
        q_out[:, c * LANES:(c + 1) * LANES] = norm_rope(
            y[:, c * LANES:(c + 1) * LANES], qg, QK_SCALE).astype(BF16)

    k = norm_rope(y[:, A_Q:A_Q + LANES], kg, 1.0)
    kr = pltpu.roll(k, HEAD_DIM, 1)
    k_out[:, 0:LANES] = jnp.where(half0, k, kr).astype(BF16)
    k_out[:, LANES:2 * LANES] = jnp.where(half0, kr, k).astype(BF16)

    v = y[:, A_Q + LANES:A_Q + 2 * LANES]
    vr = pltpu.roll(v, HEAD_DIM, 1)
    v_out[:, 0:LANES] = jnp.where(half0, v, vr).astype(BF16)
    v_out[:, LANES:2 * LANES] = jnp.where(half0, vr, v).astype(BF16)


def _norm_proj_kernel(x_ref, g_ref, w_ref, cs_ref, cos_ref, sin_ref, qg_ref, kg_ref, bd_ref,
                      o_ref, q_ref, k_ref, v_ref, c0_ref, c1_ref, c2_ref, h_ref, ys_ref,
                      *, n_gate_tiles, n_main_tiles, dils):
    j = pl.program_id(1)
    tm = x_ref.shape[0]

    @pl.when(j == 0)
    def _():
        x = x_ref[...]
        ms = jnp.mean(x * x, axis=-1, keepdims=True)
        h_ref[...] = (x * lax.rsqrt(ms + EPS) * g_ref[...]).astype(BF16)

    def project():
        return jnp.dot(h_ref[...], w_ref[...], preferred_element_type=F32) * cs_ref[...]

    @pl.when(j < n_gate_tiles)
    def _():
        o_ref[...] = jax.nn.sigmoid(project()).astype(BF16)

    @pl.when(j == n_gate_tiles)
    def _():
        y = project()
        o_ref[...] = y.astype(BF16)
        _prep_a(y, cos_ref[...], sin_ref[...], qg_ref[...], kg_ref[...], bd_ref[...], q_ref, k_ref, v_ref)

    @pl.when((j > n_gate_tiles) & (j < n_main_tiles))
    def _():
        o_ref[...] = project().astype(BF16)

    for g, (c_ref, dil) in enumerate(zip((c0_ref, c1_ref, c2_ref), dils)):
        @pl.when(j == n_main_tiles + g)
        def _(c_ref=c_ref, dil=dil):
            y = project()
            if dil == 1:
                c_ref[0, 0] = y.astype(BF16)
            else:
                for c in range(y.shape[1] // LANES):
                    ys_ref[c] = y[:, c * LANES:(c + 1) * LANES]
                for r in range(dil):
                    for c in range(y.shape[1] // LANES):
                        c_ref[0, r, :, c * LANES:(c + 1) * LANES] = ys_ref[
                            c, pl.ds(r, tm // dil, stride=dil), :].astype(BF16)


def _norm_proj(x, gain, w, colscale, cos_t, sin_t, qg, kg, bd, bsz, seq, *, tm=1024, tn=C_SLAB):
    n, d = x.shape
    tm = min(tm, seq)
    dils = tuple(dil for _, dil in C_GROUPS)
    assert seq % tm == 0 and tn == C_SLAB and MAIN_COLS % tn == 0 and GATE_COLS % tn == 0
    assert tn == A_SLAB and OFF_AQ == GATE_COLS and A_KV == LANES
    assert all(tm % (2 * SUBLANES * dil) == 0 for dil in dils) and len(dils) == 3
    n_main = MAIN_COLS // tn
    tps = seq // tm

    def c_spec(dil):
        return pl.BlockSpec((1, dil, tm // dil, tn), lambda i, j: (i // tps, 0, i % tps, 0))

    const = lambda shape: pl.BlockSpec(shape, lambda i, j: (0, 0))
    rows = lambda width: pl.BlockSpec((tm, width), lambda i, j: (i, 0))
    return pl.pallas_call(
        functools.partial(_norm_proj_kernel, n_gate_tiles=GATE_COLS // tn, n_main_tiles=n_main, dils=dils),
        grid=(n // tm, n_main + len(dils)),
        in_specs=[
            rows(d),
            const((1, d)),
            pl.BlockSpec((d, tn), lambda i, j: (0, j)),
            pl.BlockSpec((1, tn), lambda i, j: (0, j)),
            pl.BlockSpec((tm, LANES), lambda i, j: (i % tps, 0)),
            pl.BlockSpec((tm, LANES), lambda i, j: (i % tps, 0)),
            const((1, LANES)), const((1, LANES)), const((LANES, LANES)),
        ],
        out_specs=[pl.BlockSpec((tm, tn), lambda i, j: (i, jnp.minimum(j, n_main - 1))),
                   rows(A_Q), rows(2 * LANES), rows(2 * LANES)]
        + [c_spec(dil) for dil in dils],
        out_shape=[jax.ShapeDtypeStruct((n, MAIN_COLS), BF16), jax.ShapeDtypeStruct((n, A_Q), BF16),
                   jax.ShapeDtypeStruct((n, 2 * LANES), BF16), jax.ShapeDtypeStruct((n, 2 * LANES), BF16)]
        + [jax.ShapeDtypeStruct((bsz, dil, seq // dil, tn), BF16) for dil in dils],
        scratch_shapes=[pltpu.VMEM((tm, d), BF16), pltpu.VMEM((tn // LANES, tm, LANES), F32)],
        compiler_params=_cparams(("parallel", "arbitrary")),
        name="norm_proj",
    )(x, gain.reshape(1, d), w, colscale, cos_t, sin_t, qg, kg, bd)


def _colmax(s):
    tk, r = s.shape
    m8 = jnp.max(s.reshape(tk // SUBLANES, SUBLANES, r), axis=0)
    return jnp.max(m8, axis=0, keepdims=True)


def _softmax_pv(s, mb, v_ref, m_ref, l_ref, acc_ref):
    tk, r = s.shape
    m_prev = m_ref[...]
    m_cur = jnp.maximum(m_prev, mb)
    alpha = jnp.exp2(m_prev - m_cur)
    p = jnp.exp2(s - m_cur)
    l_ref[...] = alpha * l_ref[...] + jnp.sum(p.reshape(tk // SUBLANES, SUBLANES, r), axis=0)
    acc_ref[...] = alpha * acc_ref[...] + lax.dot_general(
        v_ref[...], p.astype(BF16), TN_DIMS, preferred_element_type=F32)
    m_ref[...] = m_cur


def _flash_steps(t, nk, scores_fn, init_fn, finish_fn, v_ref, s_ref, mb_ref, m_ref, l_ref, acc_ref):
    def produce(cur):
        for col, s_new in scores_fn():
            width = s_new.shape[1]
            mb_ref[cur, :, col:col + width] = _colmax(s_new)
            s_ref[cur, :, col:col + width] = s_new

    def consume(prev):
        _softmax_pv(s_ref[prev], mb_ref[prev], v_ref, m_ref, l_ref, acc_ref)

    @pl.when(t == 0)
    def _():
        m_ref[...] = jnp.full(m_ref.shape, NEG_BIG, F32)
        l_ref[...] = jnp.zeros(l_ref.shape, F32)
        acc_ref[...] = jnp.zeros(acc_ref.shape, F32)
        init_fn()
        produce(0)

    for par in (0, 1):
        @pl.when((t > 0) & (t < nk) & (t % 2 == par))
        def _(par=par):
            produce(par)
            consume(1 - par)

    @pl.when(t == nk)
    def _():
        consume((nk - 1) % 2)
        finish_fn()


def _flash_scratch(tk, r):
    return [
        pltpu.VMEM((r, LANES), BF16),
        pltpu.VMEM((2, tk, r), F32),
        pltpu.VMEM((2, 1, r), F32),
        pltpu.VMEM((1, r), F32),
        pltpu.VMEM((SUBLANES, r), F32),
        pltpu.VMEM((LANES, r), F32),
    ]


def _flash_a_kernel(q_ref, k_ref, v_ref, o_ref, qs_ref, s_ref, mb_ref, m_ref, l_ref, acc_ref, *, tq, nk):
    t = pl.program_id(3)
    grp = A_HEADS // A_KV_HEADS
    lane = lax.broadcasted_iota(jnp.int32, (tq, LANES), 1)
    half0 = lane < HEAD_DIM

    def init():
        for h in range(grp):
            c = h // 2
            x = q_ref[:, c * LANES:(c + 1) * LANES]
            keep = half0 if h % 2 == 0 else jnp.logical_not(half0)
            qs_ref[h * tq:(h + 1) * tq, :] = jnp.where(keep, x, jnp.zeros_like(x))

    def scores():
        yield 0, lax.dot_general(k_ref[...], qs_ref[...], NT_DIMS, preferred_element_type=F32)

    def finish():
        o_t = acc_ref[...] / jnp.sum(l_ref[...], axis=0, keepdims=True)
        for c in range(grp // 2):
            o0 = o_t[:, (2 * c) * tq:(2 * c + 1) * tq].T
            o1 = o_t[:, (2 * c + 1) * tq:(2 * c + 2) * tq].T
            o_ref[:, c * LANES:(c + 1) * LANES] = jnp.where(half0, o0, o1).astype(BF16)

    _flash_steps(t, nk, scores, init, finish, v_ref, s_ref, mb_ref, m_ref, l_ref, acc_ref)


def _flash_a(qn, kd, vd, bsz, seq, *, tq=1024, tk=1024):
    n = qn.shape[0]
    tk = min(tk, seq)
    assert seq % tq == 0 and seq % tk == 0
    nq, nk = seq // tq, seq // tk
    grp = A_HEADS // A_KV_HEADS
    qw = grp * HEAD_DIM
    return pl.pallas_call(
        functools.partial(_flash_a_kernel, tq=tq, nk=nk),
        grid=(bsz, A_KV_HEADS, nq, nk + 1),
        in_specs=[
            pl.BlockSpec((tq, qw), lambda b, kv, qi, t: (b * nq + qi, kv)),
            pl.BlockSpec((tk, LANES), lambda b, kv, qi, t: (b * nk + jnp.minimum(t, nk - 1), kv)),
            pl.BlockSpec((tk, LANES), lambda b, kv, qi, t: (b * nk + jnp.maximum(t - 1, 0), kv)),
        ],
        out_specs=pl.BlockSpec((tq, qw), lambda b, kv, qi, t: (b * nq + qi, kv)),
        out_shape=jax.ShapeDtypeStruct((n, A_Q), BF16),
        scratch_shapes=_flash_scratch(tk, grp * tq),
        compiler_params=_cparams(("parallel", "parallel", "parallel", "arbitrary")),
        name="flash_a",
    )(qn, kd, vd)


def _flash_b_kernel(lam_ref, q_ref, k_ref, v_ref, bias_ref, sub_ref, o_ref,
                    qs_ref, s_ref, mb_ref, m_ref, l_ref, acc_ref, *, tq, nk, out_scale):
    qi = pl.program_id(2)
    t = pl.program_id(3)
    lane = lax.broadcasted_iota(jnp.int32, (tq, LANES), 1)
    half0 = lane < HEAD_DIM

    def init():
        x = q_ref[...]
        zero = jnp.zeros_like(x)
        qs_ref[0:tq, :] = jnp.where(half0, x, zero)
        qs_ref[tq:2 * tq, :] = jnp.where(half0, zero, x)

    def scores():
        bias = bias_ref[0, jnp.clip(t - qi, -2, 2) + 2]
        k = k_ref[...]
        for mp in range(2):
            yield mp * tq, lax.dot_general(
                k, qs_ref[mp * tq:(mp + 1) * tq, :], NT_DIMS, preferred_element_type=F32) + bias

    def finish():
        lam = lam_ref[0]
        o_t = acc_ref[...] / jnp.sum(l_ref[...], axis=0, keepdims=True)
        o = (o_t[:, 0:tq] - lam * o_t[:, tq:2 * tq]).T
        ms = jnp.mean(o * o, axis=-1, keepdims=True)
        o_ref[...] = (o * lax.rsqrt(ms + EPS) * sub_ref[...] * out_scale).astype(BF16)

    _flash_steps(t, nk, scores, init, finish, v_ref, s_ref, mb_ref, m_ref, l_ref, acc_ref)


def _flash_b(y, bias_tiles, lam, subln, bsz, seq, lambda_init, *, t):
    n = y.shape[0]
    assert seq % t == 0 and bias_tiles.shape == (B_HEADS, 5, t, t)
    nq = seq // t
    qb, kb, vb = OFF_BQ // LANES, OFF_BK // LANES, OFF_BV // LANES
    return pl.pallas_call(
        functools.partial(_flash_b_kernel, tq=t, nk=nq, out_scale=1.0 - lambda_init),
        grid=(B_HEADS, bsz, nq, nq + 1),
        in_specs=[
            pl.BlockSpec(memory_space=pltpu.SMEM),
            pl.BlockSpec((t, LANES), lambda h, b, qi, ts: (b * nq + qi, qb + h)),
            pl.BlockSpec((t, LANES), lambda h, b, qi, ts: (b * nq + jnp.minimum(ts, nq - 1), kb + h)),
            pl.BlockSpec((t, LANES), lambda h, b, qi, ts: (b * nq + jnp.maximum(ts - 1, 0), vb + h)),
            pl.BlockSpec((1, 5, t, t), lambda h, b, qi, ts: (h, 0, 0, 0), pipeline_mode=pl.Buffered(1)),
            pl.BlockSpec((1, LANES), lambda h, b, qi, ts: (0, 0)),
        ],
        out_specs=pl.BlockSpec((t, LANES), lambda h, b, qi, ts: (b * nq + qi, h)),
        out_shape=jax.ShapeDtypeStruct((n, B_V), BF16),
        scratch_shapes=_flash_scratch(t, 2 * t),
        compiler_params=_cparams(("parallel", "parallel", "parallel", "arbitrary")),
        name="flash_b",
    )(lam, y, y, y, bias_tiles, subln)


def _window_kernel(q_ref, kp_ref, km_ref, kn_ref, vp_ref, vm_ref, vn_ref, bias_ref,
                   o_ref, lse_ref, *, tq, halo, length, dil):
    i = pl.program_id(1)
    nkeys = tq + 2 * halo
    ukey = i * tq - halo + lax.broadcasted_iota(jnp.int32, (nkeys, 2 * tq), 0)
    key_ok = (ukey >= 0) & (ukey < length)
    lane = lax.broadcasted_iota(jnp.int32, (tq, LANES), 1)
    half0 = lane < HEAD_DIM
    top_half = lax.broadcasted_iota(jnp.int32, (LANES, tq), 0) < HEAD_DIM

    def scores(r, c, edge):
        qc = q_ref[0, r, :, c * LANES:(c + 1) * LANES]
        kc = jnp.concatenate([ref[0, r, :, c * LANES:(c + 1) * LANES] for ref in (kp_ref, km_ref, kn_ref)], axis=0)
        zero = jnp.zeros_like(qc)
        qpair = jnp.concatenate([jnp.where(half0, qc, zero), jnp.where(half0, zero, qc)], axis=0)
        s = lax.dot_general(kc, qpair, NT_DIMS, preferred_element_type=F32) + bias_ref[c]
        if edge:
            s = jnp.where(key_ok, s, NEG_BIG)
        return s

    def softmax(s):
        m = _colmax(s)
        e = jnp.exp2(s - m)
        den = jnp.sum(e.reshape(nkeys // SUBLANES, SUBLANES, 2 * tq), axis=0)
        den = jnp.sum(den, axis=0, keepdims=True)
        return e.astype(BF16), den, (m + jnp.log2(den)) * LN2

    def values(r, c, p, den):
        vc = jnp.concatenate([ref[0, r, :, c * LANES:(c + 1) * LANES] for ref in (vp_ref, vm_ref, vn_ref)], axis=0)
        return lax.dot_general(vc, p, TN_DIMS, preferred_element_type=F32) / den

    def store(r, c, o_t, lse):
        rows = pl.ds(r, tq, stride=dil) if dil > 1 else pl.ds(0, tq)
        o_ref[c, rows, :] = jnp.where(top_half, o_t[:, 0:tq], o_t[:, tq:2 * tq]).T
        lse_ref[c, rows, :] = jnp.where(top_half, lse[:, 0:tq], lse[:, tq:2 * tq]).T

    def all_residues(edge):
        group = min(dil, 4)
        for r0 in range(0, dil, group):
            items = [(r, c) for r in range(r0, r0 + group) for c in range(C_HEADS // 2)]
            ss = [scores(r, c, edge) for r, c in items]
            sm = [softmax(s) for s in ss]
            os_ = [values(r, c, p, den) for (r, c), (p, den, _) in zip(items, sm)]
            for (r, c), o_t, (_, _, lse) in zip(items, os_, sm):
                store(r, c, o_t, lse)

    on_edge = (i == 0) | (i == pl.num_programs(1) - 1)

    @pl.when(on_edge)
    def _():
        all_residues(True)

    @pl.when(jnp.logical_not(on_edge))
    def _():
        all_residues(False)


def _window_group(cg, bias_tile, dil, bsz, seq, *, tq, halo):
    length = seq // dil
    tq = min(tq, length)
    cw = C_OUT
    assert length % tq == 0 and tq % halo == 0 and length % halo == 0
    assert bias_tile.shape == (C_HEADS // 2, tq + 2 * halo, 2 * tq)
    nt = length // tq
    hb = tq // halo
    nhb = length // halo
    main = lambda col: pl.BlockSpec((1, dil, tq, cw), lambda b, i: (b, 0, i, col))
    prev = lambda col: pl.BlockSpec(
        (1, dil, halo, cw), lambda b, i: (b, 0, jnp.maximum(i * hb - 1, 0), col))
    nxt = lambda col: pl.BlockSpec(
        (1, dil, halo, cw), lambda b, i: (b, 0, jnp.minimum((i + 1) * hb, nhb - 1), col))
    out = pl.BlockSpec((cw // LANES, tq * dil, LANES), lambda b, i: (0, b * nt + i, 0))
    return pl.pallas_call(
        functools.partial(_window_kernel, tq=tq, halo=halo, length=length, dil=dil),
        grid=(bsz, nt),
        in_specs=[main(0), prev(1), main(1), nxt(1), prev(2), main(2), nxt(2),
                  pl.BlockSpec(bias_tile.shape, lambda b, i: (0, 0, 0))],
        out_specs=[out, out],
        out_shape=[jax.ShapeDtypeStruct((cw // LANES, bsz * seq, LANES), F32)] * 2,
        compiler_params=_cparams(("parallel", "parallel")),
        name=f"window_d{dil}",
    )(cg, cg, cg, cg, cg, cg, cg, bias_tile)


def _merge_kernel(x_ref, oa_ref, ob_ref, oc0_ref, oc1_ref, oc2_ref, ls0_ref, ls1_ref, ls2_ref,
                  g0_ref, g1_ref, g2_ref, wb_ref, wo_ref, o_ref):
    planes = lambda ref: jnp.concatenate([ref[c] for c in range(C_OUT // LANES)], axis=1)
    ls = [planes(ls0_ref), planes(ls1_ref), planes(ls2_ref)]
    ocs = [planes(oc0_ref), planes(oc1_ref), planes(oc2_ref)]
    m = jnp.maximum(jnp.maximum(ls[0], ls[1]), ls[2])
    es = [jnp.exp(l - m) for l in ls]
    den = es[0] + es[1] + es[2]
    oc = (es[0] * ocs[0] + es[1] * ocs[1] + es[2] * ocs[2]) / den
    ya = jnp.dot(oa_ref[...], wb_ref[0:A_Q, :], preferred_element_type=F32)
    yb = jnp.dot(ob_ref[...], wb_ref[A_Q:A_Q + B_V, :], preferred_element_type=F32)
    yc = jnp.dot(oc.astype(BF16), wb_ref[A_Q + B_V:, :], preferred_element_type=F32)
    merged = (g0_ref[...].astype(F32) * ya + g1_ref[...].astype(F32) * yb
              + g2_ref[...].astype(F32) * yc)
    o_ref[...] = x_ref[...] + jnp.dot(merged.astype(BF16), wo_ref[...], preferred_element_type=F32)


def _merge(x, oa, ob, ocs, lss, y, wb, wo, *, tm=512):
    n, d = x.shape
    assert n % tm == 0 and OFF_GATE == 0
    row = lambda w: pl.BlockSpec((tm, w), lambda i: (i, 0))
    gate = lambda j: pl.BlockSpec((tm, d), lambda i: (i, j))
    plane = pl.BlockSpec((C_OUT // LANES, tm, LANES), lambda i: (0, i, 0))
    return pl.pallas_call(
        _merge_kernel,
        grid=(n // tm,),
        in_specs=[row(d), row(A_Q), row(B_V)] + [plane] * 6 + [gate(0), gate(1), gate(2)] + [
            pl.BlockSpec(wb.shape, lambda i: (0, 0)),
            pl.BlockSpec(wo.shape, lambda i: (0, 0)),
        ],
        out_specs=row(d),
        out_shape=jax.ShapeDtypeStruct((n, d), F32),
        compiler_params=_cparams(("parallel",)),
        name="merge",
    )(x, oa, ob, *ocs, *lss, y, y, y, wb, wo)


def _rms(x, gain):
    ms = jnp.mean(x * x, axis=-1, keepdims=True)
    return x * lax.rsqrt(ms + EPS) * gain


def _finish(x_ref, acc_ref, o_ref, fg_ref):
    out = x_ref[...] + acc_ref[...]
    if fg_ref is not None:
        out = _rms(out, fg_ref[...])
    o_ref[...] = out


def _ffn_kernel(x_ref, g_ref, w1_ref, w3_ref, w2_ref, *rest, nf, final_norm):
    if final_norm:
        fg_ref, o_ref, h_ref, acc_ref = rest
    else:
        fg_ref = None
        o_ref, h_ref, acc_ref = rest
    j = pl.program_id(1)

    @pl.when(j == 0)
    def _():
        h_ref[...] = _rms(x_ref[...], g_ref[...]).astype(BF16)
        acc_ref[...] = jnp.zeros(acc_ref.shape, F32)

    h = h_ref[...]
    a = jnp.dot(h, w1_ref[...], preferred_element_type=F32)
    b = jnp.dot(h, w3_ref[...], preferred_element_type=F32)
    t = (jax.nn.silu(a) * b).astype(BF16)
    acc_ref[...] += jnp.dot(t, w2_ref[...], preferred_element_type=F32)

    @pl.when(j == nf - 1)
    def _():
        _finish(x_ref, acc_ref, o_ref, fg_ref)


def _ffn(x, gain, w1, w3, w2, final_gain=None, *, tm=512, tf=1408):
    n, d = x.shape
    ff = w1.shape[1]
    assert n % tm == 0 and ff % tf == 0
    nf = ff // tf
    final_norm = final_gain is not None
    in_specs = [
        pl.BlockSpec((tm, d), lambda i, j: (i, 0)),
        pl.BlockSpec((1, d), lambda i, j: (0, 0)),
        pl.BlockSpec((d, tf), lambda i, j: (0, j)),
        pl.BlockSpec((d, tf), lambda i, j: (0, j)),
        pl.BlockSpec((tf, d), lambda i, j: (j, 0)),
    ]
    args = [x, gain.reshape(1, d), w1, w3, w2]
    if final_norm:
        in_specs.append(pl.BlockSpec((1, d), lambda i, j: (0, 0)))
        args.append(final_gain.reshape(1, d))
    return pl.pallas_call(
        functools.partial(_ffn_kernel, nf=nf, final_norm=final_norm),
        grid=(n // tm, nf),
        in_specs=in_specs,
        out_specs=pl.BlockSpec((tm, d), lambda i, j: (i, 0)),
        out_shape=jax.ShapeDtypeStruct((n, d), F32),
        scratch_shapes=[pltpu.VMEM((tm, d), BF16), pltpu.VMEM((tm, d), F32)],
        compiler_params=_cparams(("parallel", "arbitrary")),
        name="ffn",
    )(*args)


def _top2_gate(logits):
    lane = lax.broadcasted_iota(jnp.int32, logits.shape, 1)
    m1 = jnp.max(logits, axis=-1, keepdims=True)
    i1 = jnp.min(jnp.where(logits == m1, lane, LANES), axis=-1, keepdims=True)
    rest = jnp.where(lane == i1, -jnp.inf, logits)
    m2 = jnp.max(rest, axis=-1, keepdims=True)
    i2 = jnp.min(jnp.where(rest == m2, lane, LANES), axis=-1, keepdims=True)
    e2 = jnp.exp(m2 - m1)
    w1 = 1.0 / (1.0 + e2)
    w2 = e2 / (1.0 + e2)
    return jnp.where(lane == i1, w1, 0.0) + jnp.where(lane == i2, w2, 0.0)


def _moe_route_kernel(x_ref, g_ref, r_ref, tri_ref, cum_ref, h_ref, gate_ref, rank_ref, mask_ref, cnt_ref):
    hf = _rms(x_ref[...], g_ref[...])
    h_ref[...] = hf.astype(BF16)
    logits = jnp.dot(hf, r_ref[...], preferred_element_type=F32, precision=lax.Precision.HIGHEST)
    lane = lax.broadcasted_iota(jnp.int32, logits.shape, 1)
    gate = _top2_gate(jnp.where(lane < N_EXPERTS, logits, -jnp.inf))
    gate_ref[...] = gate
    sel = jnp.where(gate.T[0:N_EXPERTS, :] != 0.0, 1.0, 0.0)
    mask_ref[0] = sel
    selb = sel.astype(BF16)
    rank_ref[0] = jnp.dot(selb, tri_ref[...], preferred_element_type=F32)
    cnt_ref[0] = jnp.dot(selb, cum_ref[...], preferred_element_type=F32).astype(jnp.int32)


def _moe_route(x, gain, router_p, tri, cum, *, tm):
    n, d = x.shape
    nt = n // tm
    full = lambda shape: pl.BlockSpec(shape, lambda i: (0,) * len(shape))
    per_tile = lambda shape: pl.BlockSpec((1,) + shape, lambda i: (i, 0, 0))
    return pl.pallas_call(
        _moe_route_kernel,
        grid=(nt,),
        in_specs=[pl.BlockSpec((tm, d), lambda i: (i, 0)), full((1, d)), full((d, LANES)),
                  full((tm, tm)), full((tm, LANES))],
        out_specs=[pl.BlockSpec((tm, d), lambda i: (i, 0)), pl.BlockSpec((tm, LANES), lambda i: (i, 0)),
                   per_tile((N_EXPERTS, tm)), per_tile((N_EXPERTS, tm)), per_tile((N_EXPERTS, LANES))],
        out_shape=[jax.ShapeDtypeStruct((n, d), BF16), jax.ShapeDtypeStruct((n, LANES), F32),
                   jax.ShapeDtypeStruct((nt, N_EXPERTS, tm), F32),
                   jax.ShapeDtypeStruct((nt, N_EXPERTS, tm), F32),
                   jax.ShapeDtypeStruct((nt, N_EXPERTS, LANES), jnp.int32)],
        compiler_params=_cparams(("parallel",)),
        name="moe_route",
    )(x, gain.reshape(1, d), router_p, tri, cum)


def _moe_experts_kernel(cnt_ref, h_ref, gate_ref, rank_ref, mask_ref, w1_ref, w3_ref, w2_ref, o_ref,
                        hc_ref, oc_ref, *, nf, nb, ch, wb, fb):
    i, e, j = pl.program_id(0), pl.program_id(1), pl.program_id(2)
    tm, d = h_ref.shape
    n_rows = cnt_ref[i, e, nb]
    n_chunks = (n_rows + ch - 1) // ch

    def onehot(c, b):
        rank = rank_ref[0, pl.ds(e, 1), b * wb:(b + 1) * wb]
        sel = mask_ref[0, pl.ds(e, 1), b * wb:(b + 1) * wb]
        srow = (c * ch + lax.broadcasted_iota(jnp.int32, (ch, 1), 0)).astype(F32)
        return jnp.where((rank == srow) & (sel > 0.5), 1.0, 0.0).astype(BF16)

    def hits(c, b):
        return (cnt_ref[i, e, b] < (c + 1) * ch) & (cnt_ref[i, e, b + 1] > c * ch)

    @pl.when((e == 0) & (j == 0))
    def _():
        o_ref[...] = jnp.zeros(o_ref.shape, F32)

    @pl.when(j == 0)
    def _():
        def gather(c, carry):
            rows = pl.ds(pl.multiple_of(c * ch, ch), ch)
            hc_ref[rows, :] = jnp.zeros((ch, d), BF16)
            for b in range(nb):
                @pl.when(hits(c, b))
                def _(b=b):
                    picked = jnp.dot(onehot(c, b), h_ref[b * wb:(b + 1) * wb, :], preferred_element_type=F32)
                    hc_ref[rows, :] += picked.astype(BF16)
            return carry
        lax.fori_loop(0, n_chunks, gather, 0)

    def expert(row0, size):
        rows = pl.ds(pl.multiple_of(row0, size), size)
        hcb = hc_ref[rows, :]
        a = jnp.dot(hcb, w1_ref[0], preferred_element_type=F32)
        b3 = jnp.dot(hcb, w3_ref[0], preferred_element_type=F32)
        r = jnp.dot((jax.nn.silu(a) * b3).astype(BF16), w2_ref[0], preferred_element_type=F32)

        @pl.when(j == 0)
        def _():
            oc_ref[rows, :] = r

        @pl.when(j > 0)
        def _():
            oc_ref[rows, :] += r

    def expert_body(f, carry):
        expert(f * fb, fb)
        return carry
    n_full = (n_chunks * ch) // fb
    lax.fori_loop(0, n_full, expert_body, 0)

    @pl.when(n_chunks * ch > n_full * fb)
    def _():
        expert(n_full * fb, ch)

    @pl.when(j == nf - 1)
    def _():
        def scatter(c, carry):
            ocb = oc_ref[pl.ds(pl.multiple_of(c * ch, ch), ch), :].astype(BF16)
            for b in range(nb):
                @pl.when(hits(c, b))
                def _(b=b):
                    back = lax.dot_general(onehot(c, b), ocb, TN_DIMS, preferred_element_type=F32)
                    g = gate_ref[b * wb:(b + 1) * wb, :]
                    lane = lax.broadcasted_iota(jnp.int32, g.shape, 1)
                    gcol = jnp.sum(jnp.where(lane == e, g, 0.0), axis=-1, keepdims=True)
                    o_ref[b * wb:(b + 1) * wb, :] += gcol * back
            return carry
        lax.fori_loop(0, n_chunks, scatter, 0)


def _moe_experts(cnt, h, gate, rank, mask, w1, w3, w2, *, tm, tf=1408, ch=128, wb=256, fb=256):
    n, d = h.shape
    ne, _, ff = w1.shape
    assert n % tm == 0 and ff % tf == 0 and tm % wb == 0 and tm % fb == 0 and fb == 2 * ch
    nf, nb = ff // tf, tm // wb
    assert cnt.shape == (n // tm, ne, nb + 1)
    grid_spec = pltpu.PrefetchScalarGridSpec(
        num_scalar_prefetch=1,
        grid=(n // tm, ne, nf),
        in_specs=[
            pl.BlockSpec((tm, d), lambda i, e, j, c: (i, 0), pipeline_mode=pl.Buffered(1)),
            pl.BlockSpec((tm, LANES), lambda i, e, j, c: (i, 0)),
            pl.BlockSpec((1, ne, tm), lambda i, e, j, c: (i, 0, 0)),
            pl.BlockSpec((1, ne, tm), lambda i, e, j, c: (i, 0, 0)),
            pl.BlockSpec((1, d, tf), lambda i, e, j, c: (e, 0, j)),
            pl.BlockSpec((1, d, tf), lambda i, e, j, c: (e, 0, j)),
            pl.BlockSpec((1, tf, d), lambda i, e, j, c: (e, j, 0)),
        ],
        out_specs=pl.BlockSpec((tm, d), lambda i, e, j, c: (i, 0), pipeline_mode=pl.Buffered(1)),
        scratch_shapes=[pltpu.VMEM((tm, d), BF16), pltpu.VMEM((tm, d), F32)],
    )
    return pl.pallas_call(
        functools.partial(_moe_experts_kernel, nf=nf, nb=nb, ch=ch, wb=wb, fb=fb),
        grid_spec=grid_spec,
        out_shape=jax.ShapeDtypeStruct((n, d), F32),
        compiler_params=_cparams(("parallel", "arbitrary", "arbitrary")),
        name="moe_experts",
    )(cnt, h, gate, rank, mask, w1, w3, w2)


def _residual_kernel(x_ref, y_ref, *rest, final_norm):
    if final_norm:
        fg_ref, o_ref = rest
    else:
        fg_ref = None
        (o_ref,) = rest
    _finish(x_ref, y_ref, o_ref, fg_ref)


def _residual(x, y, final_gain=None, *, tm=1024):
    n, d = x.shape
    assert n % tm == 0
    final_norm = final_gain is not None
    row = pl.BlockSpec((tm, d), lambda i: (i, 0))
    in_specs, args = [row, row], [x, y]
    if final_norm:
        in_specs.append(pl.BlockSpec((1, d), lambda i: (0, 0)))
        args.append(final_gain.reshape(1, d))
    return pl.pallas_call(
        functools.partial(_residual_kernel, final_norm=final_norm),
        grid=(n // tm,),
        in_specs=in_specs,
        out_specs=row,
        out_shape=jax.ShapeDtypeStruct((n, d), F32),
        compiler_params=_cparams(("parallel",)),
        name="residual",
    )(*args)


def _moe(x, gain, router_p, w1, w3, w2, final_gain=None, *, tm=2048, wb=256):
    n = x.shape[0]
    tm = min(tm, n)
    t = jnp.arange(tm)
    tri = (t[:, None] < t[None, :]).astype(BF16)
    cum = (t[:, None] < jnp.arange(LANES)[None, :] * wb).astype(BF16)
    h, gate, rank, mask, cnt = _moe_route(x, gain, router_p, tri, cum, tm=tm)
    y = _moe_experts(cnt[:, :, :tm // wb + 1], h, gate, rank, mask, w1, w3, w2, tm=tm, wb=wb)
    return _residual(x, y, final_gain)


def _t5_bucket(rel):
    nb = NUM_BUCKETS // 2
    max_exact = nb // 2
    ret = jnp.where(rel > 0, nb, 0)
    n = jnp.abs(rel)
    large = max_exact + (jnp.log(jnp.maximum(n, max_exact).astype(F32) / max_exact)
                         / math.log(MAX_DISTANCE / max_exact) * (nb - max_exact)).astype(jnp.int32)
    large = jnp.minimum(large, nb - 1)
    return ret + jnp.where(n < max_exact, n, large)


def _toeplitz_kernel(w_ref, o_ref):
    rows, cols = o_ref.shape[1:]
    w = jnp.broadcast_to(w_ref[0], (rows, w_ref.shape[2]))
    o_ref[0] = pltpu.roll(w, 0, 1, stride=1, stride_axis=0)[:, :cols]


def _toeplitz(value_of_rel, rows, cols, col0):
    period = -(-(rows + cols) // LANES) * LANES
    k = np.arange(period)
    rel = jnp.asarray(col0 + np.where(k < cols, k, k - period), jnp.int32)
    w = value_of_rel(rel).astype(F32)
    lead = w.shape[:-1]
    n = int(np.prod(lead)) if lead else 1
    out = pl.pallas_call(
        _toeplitz_kernel,
        grid=(n,),
        in_specs=[pl.BlockSpec((1, 1, period), lambda i: (i, 0, 0))],
        out_specs=pl.BlockSpec((1, rows, cols), lambda i: (i, 0, 0)),
        out_shape=jax.ShapeDtypeStruct((n, rows, cols), F32),
        compiler_params=_cparams(("parallel",)),
        name="toeplitz",
    )(w.reshape(n, 1, period))
    return out.reshape(lead + (rows, cols))


def _rope_tables(seq):
    t = jnp.arange(seq)
    row = (t // GRID_W).astype(F32)
    col = (t % GRID_W).astype(F32)
    half = HEAD_DIM // 2
    inv = ROPE_THETA ** (-jnp.arange(0, half, 2, dtype=F32) / half)
    ar = row[:, None] * inv
    ac = col[:, None] * inv
    cos = jnp.concatenate([jnp.cos(ar), jnp.cos(ar), jnp.cos(ac), jnp.cos(ac)], axis=1)
    sin = jnp.concatenate([-jnp.sin(ar), jnp.sin(ar), -jnp.sin(ac), jnp.sin(ac)], axis=1)
    return jnp.tile(cos, (1, LANES // HEAD_DIM)), jnp.tile(sin, (1, LANES // HEAD_DIM))


def _bias_tiles_b(rel_bias, t):
    cols = rel_bias[:, :B_HEADS].astype(F32).T * LOG2E
    by_rel = lambda rel: cols[:, _t5_bucket(rel)]
    tiles = []
    for delta in (-2, -1, 0, 1, 2):
        if abs(delta) == 2:
            far = by_rel(jnp.full((1,), delta * t, jnp.int32))
            tiles.append(jnp.broadcast_to(far[:, :, None], (B_HEADS, t, t)))
        else:
            tiles.append(_toeplitz(lambda r: by_rel(-r), t, t, -delta * t))
    return jnp.stack(tiles, axis=1)


def _bias_tile_c(rel_bias, g, dil, span, tq, halo):
    cols = rel_bias[:, B_HEADS + g * C_HEADS:B_HEADS + (g + 1) * C_HEADS].astype(F32).T * LOG2E
    nkeys = tq + 2 * halo

    def by_rel(rel):
        return jnp.where((jnp.abs(rel) <= span)[None, :], cols[:, _t5_bucket(rel * dil)], NEG_BIG)

    per_head = _toeplitz(lambda r: by_rel(-r), nkeys, tq, halo)
    pairs = per_head.reshape(C_HEADS // 2, 2, nkeys, tq).transpose(0, 2, 1, 3)
    return pairs.reshape(C_HEADS // 2, nkeys, 2 * tq)


def _layout_w_in(w):
    a0 = 0
    b0 = a0 + A_SLAB
    c0 = b0 + 2 * B_QK + B_V
    g0 = c0 + 3 * C_QKV
    parts = [w[..., g0:g0 + GATE_COLS], w[..., a0:b0], w[..., b0:c0]]
    for g in range(N_CGROUPS):
        for kind in range(3):
            s = c0 + kind * C_QKV + g * C_OUT
            parts.append(w[..., s:s + C_OUT])
    return jnp.concatenate(parts, axis=-1)


def _trunk(x3, p):
    bsz, seq, d = x3.shape
    n = bsz * seq
    x = x3.reshape(n, d)
    depth = p["ln1"].shape[0]
    assert seq >= MAX_DISTANCE and depth > 0
    cos_t, sin_t = _rope_tables(seq)
    tb = min(1024, seq)
    bias_b = _bias_tiles_b(p["rel_bias"], tb)
    win_halo = 64
    win_tq = [min(512, seq // dil, 4096 // dil) for _, dil in C_GROUPS]
    bias_c = []
    for g, (window, dil) in enumerate(C_GROUPS):
        span = window // (2 * dil)
        assert span <= win_halo
        bias_c.append(_bias_tile_c(p["rel_bias"], g, dil, span, win_tq[g], win_halo))

    for l in range(depth):
        y, qn, kd, vd, *cgs = _norm_proj(x, p["ln1"][l], p["w_in"][l], p["colscale"], cos_t, sin_t,
                                         p["qg"][l], p["kg"][l], p["bd"], bsz, seq)
        oa = _flash_a(qn, kd, vd, bsz, seq)
        lambda_init = 0.8 - 0.6 * math.exp(-0.3 * l)
        ob = _flash_b(y, bias_b, p["lam"][l], p["subln"][l], bsz, seq, lambda_init, t=tb)
        ocs, lss = [], []
        for g, (window, dil) in enumerate(C_GROUPS):
            oc, ls = _window_group(cgs[g], bias_c[g], dil, bsz, seq, tq=win_tq[g], halo=win_halo)
            ocs.append(oc)
            lss.append(ls)
        x = _merge(x, oa, ob, ocs, lss, y, p["w_br"][l], p["w_o"][l])
        fg = p["ln_f"] if l == depth - 1 else None
        jj = l // 2
        if l % 2 == 0:
            x = _ffn(x, p["ln2"][l], p["ffn_w1"][jj], p["ffn_w3"][jj], p["ffn_w2"][jj], fg)
        else:
            x = _moe(x, p["ln2"][l], p["router"][jj], p["exp_w1"][jj], p["exp_w3"][jj],
                     p["exp_w2"][jj], fg)
    return x.reshape(bsz, seq, d)


def kernel(x_prompt, x_sample, rel_bias, ln1, ln2, ln_f, w_in, a_qnorm, a_knorm, lam_q1, lam_k1,
           lam_q2, lam_k2, b_subln, w_br, w_o, ffn_w1, ffn_w3, ffn_w2, router, exp_w1, exp_w3, exp_w2):
    depth = ln1.shape[0]
    colscale = np.ones((1, IN_COLS), np.float32)
    colscale[:, OFF_BQ:OFF_BQ + B_QK] = QK_SCALE
    for g in range(N_CGROUPS):
        colscale[:, MAIN_COLS + g * C_SLAB:MAIN_COLS + g * C_SLAB + C_OUT] = QK_SCALE
    lam = (jnp.exp(jnp.sum(lam_q1.astype(F32) * lam_k1.astype(F32), axis=-1))
           - jnp.exp(jnp.sum(lam_q2.astype(F32) * lam_k2.astype(F32), axis=-1)))
    lam = lam + jnp.asarray([0.8 - 0.6 * math.exp(-0.3 * l) for l in range(depth)], F32)
    head_id = np.arange(LANES) // HEAD_DIM
    p = {
        "rel_bias": rel_bias,
        "ln1": ln1, "ln2": ln2, "ln_f": ln_f,
        "w_in": _layout_w_in(w_in).astype(BF16),
        "colscale": jnp.asarray(colscale),
        "qg": jnp.tile(a_qnorm.astype(F32), (1, LANES // HEAD_DIM)).reshape(depth, 1, LANES),
        "kg": jnp.tile(a_knorm.astype(F32), (1, LANES // HEAD_DIM)).reshape(depth, 1, LANES),
        "bd": jnp.asarray(head_id[:, None] == head_id[None, :], F32),
        "lam": lam.reshape(depth, 1),
        "subln": b_subln.astype(F32).reshape(depth, 1, 2 * HEAD_DIM),
        "w_br": w_br.astype(BF16), "w_o": w_o.astype(BF16),
        "ffn_w1": ffn_w1.astype(BF16), "ffn_w3": ffn_w3.astype(BF16), "ffn_w2": ffn_w2.astype(BF16),
        "router": jnp.pad(router.astype(F32), ((0, 0), (0, 0), (0, LANES - N_EXPERTS))),
        "exp_w1": exp_w1.astype(BF16), "exp_w3": exp_w3.astype(BF16), "exp_w2": exp_w2.astype(BF16),
    }
    return (_trunk(x_prompt, p), _trunk(x_sample, p))
```

```python
import functools
import math

import jax
import jax.numpy as jnp
import numpy as np
from jax import lax
from jax.experimental import pallas as pl
from jax.experimental.pallas import tpu as pltpu

F32 = jnp.float32
BF16 = jnp.bfloat16

D_MODEL = 1024
HEAD_DIM = 64
GRID_W = 64
ROPE_THETA = 10000.0
EPS = 1e-6
A_HEADS = 8
A_KV_HEADS = 2
B_HEADS = 4
C_GROUPS = ((128, 1), (512, 4), (2048, 16))
C_HEADS = 4
NUM_BUCKETS = 32
MAX_DISTANCE = 128
N_EXPERTS = 8
TOP_K = 2
N_BRANCH = 3

A_Q = A_HEADS * HEAD_DIM
A_KV = A_KV_HEADS * HEAD_DIM
B_QK = B_HEADS * 2 * HEAD_DIM
B_V = B_HEADS * 2 * HEAD_DIM
N_CGROUPS = len(C_GROUPS)
C_QKV = N_CGROUPS * C_HEADS * HEAD_DIM
C_OUT = C_HEADS * HEAD_DIM
GATE_COLS = N_BRANCH * D_MODEL
IN_COLS = A_Q + 2 * A_KV + 2 * B_QK + B_V + 3 * C_QKV + GATE_COLS

OFF_GATE = 0
OFF_AQ = GATE_COLS
A_SLAB = A_Q + 2 * A_KV
OFF_BQ = OFF_AQ + A_SLAB
OFF_BK = OFF_BQ + B_QK
OFF_BV = OFF_BK + B_QK
MAIN_COLS = OFF_BV + B_V
C_SLAB = 3 * C_OUT

LANES = 128
SUBLANES = 8
VMEM_LIMIT = 56 * 1024 * 1024
LOG2E = 1.4426950408889634
LN2 = 0.6931471805599453
QK_SCALE = HEAD_DIM ** -0.5 * LOG2E
NEG_BIG = -1e30

NT_DIMS = (((1,), (1,)), ((), ()))
TN_DIMS = (((0,), (0,)), ((), ()))


def _cparams(sem):
    return pltpu.CompilerParams(dimension_semantics=sem, vmem_limit_bytes=VMEM_LIMIT)


def _prep_a(y, cos, sin, qg, kg, bd, q_out, k_out, v_out):
    tm = y.shape[0]
    lane = lax.broadcasted_iota(jnp.int32, (tm, LANES), 1)
    first16 = (lane % 32) < 16
    half0 = lane < HEAD_DIM

    def norm_rope(x, gain, out_scale):
        ss = jnp.dot(x * x, bd, preferred_element_type=F32, precision=lax.Precision.HIGHEST)
        xn = x * lax.rsqrt(ss * (1.0 / HEAD_DIM) + EPS) * gain
        sw = jnp.where(first16, pltpu.roll(xn, LANES - 16, 1), pltpu.roll(xn, 16, 1))
        return (xn * cos + sw * sin) * out_scale

    for c in range(A_Q // LANES):
        q_out[:, c * LANES:(c + 1) * LANES] = norm_rope(
            y[:, c * LANES:(c + 1) * LANES], qg, QK_SCALE).astype(BF16)

    k = norm_rope(y[:, A_Q:A_Q + LANES], kg, 1.0)
    kr = pltpu.roll(k, HEAD_DIM, 1)
    k_out[:, 0:LANES] = jnp.where(half0, k, kr).astype(BF16)
    k_out[:, LANES:2 * LANES] = jnp.where(half0, kr, k).astype(BF16)

    v = y[:, A_Q + LANES:A_Q + 2 * LANES]
    vr = pltpu.roll(v, HEAD_DIM, 1)
    v_out[:, 0:LANES] = jnp.where(half0, v, vr).astype(BF16)
    v_out[:, LANES:2 * LANES] = jnp.where(half0, vr, v).astype(BF16)


def _norm_proj_kernel(x_ref, g_ref, w_ref, cs_ref, cos_ref, sin_ref, qg_ref, kg_ref, bd_ref,
                      o_ref, q_ref, k_ref, v_ref, c0_ref, c1_ref, c2_ref, h_ref, ys_ref,
                      *, n_gate_tiles, n_main_tiles, dils):
    j = pl.program_id(1)
    tm = x_ref.shape[0]

    @pl.when(j == 0)
    def _():
        x = x_ref[...]
        ms = jnp.mean(x * x, axis=-1, keepdims=True)
        h_ref[...] = (x * lax.rsqrt(ms + EPS) * g_ref[...]).astype(BF16)

    def project():
        return jnp.dot(h_ref[...], w_ref[...], preferred_element_type=F32) * cs_ref[...]

    @pl.when(j < n_gate_tiles)
    def _():
        o_ref[...] = jax.nn.sigmoid(project()).astype(BF16)

    @pl.when(j == n_gate_tiles)
    def _():
        y = project()
        o_ref[...] = y.astype(BF16)
        _prep_a(y, cos_ref[...], sin_ref[...], qg_ref[...], kg_ref[...], bd_ref[...], q_ref, k_ref, v_ref)

    @pl.when((j > n_gate_tiles) & (j < n_main_tiles))
    def _():
        o_ref[...] = project().astype(BF16)

    for g, (c_ref, dil) in enumerate(zip((c0_ref, c1_ref, c2_ref), dils)):
        @pl.when(j == n_main_tiles + g)
        def _(c_ref=c_ref, dil=dil):
            y = project()
            if dil == 1:
                c_ref[0, 0] = y.astype(BF16)
            else:
                for c in range(y.shape[1] // LANES):
                    ys_ref[c] = y[:, c * LANES:(c + 1) * LANES]
                for r in range(dil):
                    for c in range(y.shape[1] // LANES):
                        c_ref[0, r, :, c * LANES:(c + 1) * LANES] = ys_ref[
                            c, pl.ds(r, tm // dil, stride=dil), :].astype(BF16)


def _norm_proj(x, gain, w, colscale, cos_t, sin_t, qg, kg, bd, bsz, seq, *, tm=1024, tn=C_SLAB):
    n, d = x.shape
    tm = min(tm, seq)
    dils = tuple(dil for _, dil in C_GROUPS)
    assert seq % tm == 0 and tn == C_SLAB and MAIN_COLS % tn == 0 and GATE_COLS % tn == 0
    assert tn == A_SLAB and OFF_AQ == GATE_COLS and A_KV == LANES
    assert all(tm % (2 * SUBLANES * dil) == 0 for dil in dils) and len(dils) == 3
    n_main = MAIN_COLS // tn
    tps = seq // tm

    def c_spec(dil):
        return pl.BlockSpec((1, dil, tm // dil, tn), lambda i, j: (i // tps, 0, i % tps, 0))

    const = lambda shape: pl.BlockSpec(shape, lambda i, j: (0, 0))
    rows = lambda width: pl.BlockSpec((tm, width), lambda i, j: (i, 0))
    return pl.pallas_call(
        functools.partial(_norm_proj_kernel, n_gate_tiles=GATE_COLS // tn, n_main_tiles=n_main, dils=dils),
        grid=(n // tm, n_main + len(dils)),
        in_specs=[
            rows(d),
            const((1, d)),
            pl.BlockSpec((d, tn), lambda i, j: (0, j)),
            pl.BlockSpec((1, tn), lambda i, j: (0, j)),
            pl.BlockSpec((tm, LANES), lambda i, j: (i % tps, 0)),
            pl.BlockSpec((tm, LANES), lambda i, j: (i % tps, 0)),
            const((1, LANES)), const((1, LANES)), const((LANES, LANES)),
        ],
        out_specs=[pl.BlockSpec((tm, tn), lambda i, j: (i, jnp.minimum(j, n_main - 1))),
                   rows(A_Q), rows(2 * LANES), rows(2 * LANES)]
        + [c_spec(dil) for dil in dils],
        out_shape=[jax.ShapeDtypeStruct((n, MAIN_COLS), BF16), jax.ShapeDtypeStruct((n, A_Q), BF16),
                   jax.ShapeDtypeStruct((n, 2 * LANES), BF16), jax.ShapeDtypeStruct((n, 2 * LANES), BF16)]
        + [jax.ShapeDtypeStruct((bsz, dil, seq // dil, tn), BF16) for dil in dils],
        scratch_shapes=[pltpu.VMEM((tm, d), BF16), pltpu.VMEM((tn // LANES, tm, LANES), F32)],
        compiler_params=_cparams(("parallel", "arbitrary")),
        name="norm_proj",
    )(x, gain.reshape(1, d), w, colscale, cos_t, sin_t, qg, kg, bd)


def _colmax(s):
    tk, r = s.shape
    m8 = jnp.max(s.reshape(tk // SUBLANES, SUBLANES, r), axis=0)
    return jnp.max(m8, axis=0, keepdims=True)


def _softmax_pv(s, mb, v_ref, m_ref, l_ref, acc_ref):
    tk, r = s.shape
    m_prev = m_ref[...]
    m_cur = jnp.maximum(m_prev, mb)
    alpha = jnp.exp2(m_prev - m_cur)
    p = jnp.exp2(s - m_cur)
    l_ref[...] = alpha * l_ref[...] + jnp.sum(p.reshape(tk // SUBLANES, SUBLANES, r), axis=0)
    acc_ref[...] = alpha * acc_ref[...] + lax.dot_general(
        v_ref[...], p.astype(BF16), TN_DIMS, preferred_element_type=F32)
    m_ref[...] = m_cur


def _flash_steps(t, nk, scores_fn, init_fn, finish_fn, v_ref, s_ref, mb_ref, m_ref, l_ref, acc_ref):
    def produce(cur):
        for col, s_new in scores_fn():
            width = s_new.shape[1]
            mb_ref[cur, :, col:col + width] = _colmax(s_new)
            s_ref[cur, :, col:col + width] = s_new

    def consume(prev):
        _softmax_pv(s_ref[prev], mb_ref[prev], v_ref, m_ref, l_ref, acc_ref)

    @pl.when(t == 0)
    def _():
        m_ref[...] = jnp.full(m_ref.shape, NEG_BIG, F32)
        l_ref[...] = jnp.zeros(l_ref.shape, F32)
        acc_ref[...] = jnp.zeros(acc_ref.shape, F32)
        init_fn()
        produce(0)

    for par in (0, 1):
        @pl.when((t > 0) & (t < nk) & (t % 2 == par))
        def _(par=par):
            produce(par)
            consume(1 - par)

    @pl.when(t == nk)
    def _():
        consume((nk - 1) % 2)
        finish_fn()


def _flash_scratch(tk, r):
    return [
        pltpu.VMEM((r, LANES), BF16),
        pltpu.VMEM((2, tk, r), F32),
        pltpu.VMEM((2, 1, r), F32),
        pltpu.VMEM((1, r), F32),
        pltpu.VMEM((SUBLANES, r), F32),
        pltpu.VMEM((LANES, r), F32),
    ]


def _flash_a_kernel(q_ref, k_ref, v_ref, o_ref, qs_ref, s_ref, mb_ref, m_ref, l_ref, acc_ref, *, tq, nk):
    t = pl.program_id(3)
    grp = A_HEADS // A_KV_HEADS
    lane = lax.broadcasted_iota(jnp.int32, (tq, LANES), 1)
    half0 = lane < HEAD_DIM

    def init():
        for h in range(grp):
            c = h // 2
            x = q_ref[:, c * LANES:(c + 1) * LANES]
            keep = half0 if h % 2 == 0 else jnp.logical_not(half0)
            qs_ref[h * tq:(h + 1) * tq, :] = jnp.where(keep, x, jnp.zeros_like(x))

    def scores():
        yield 0, lax.dot_general(k_ref[...], qs_ref[...], NT_DIMS, preferred_element_type=F32)

    def finish():
        o_t = acc_ref[...] / jnp.sum(l_ref[...], axis=0, keepdims=True)
        for c in range(grp // 2):
            o0 = o_t[:, (2 * c) * tq:(2 * c + 1) * tq].T
            o1 = o_t[:, (2 * c + 1) * tq:(2 * c + 2) * tq].T
            o_ref[:, c * LANES:(c + 1) * LANES] = jnp.where(half0, o0, o1).astype(BF16)

    _flash_steps(t, nk, scores, init, finish, v_ref, s_ref, mb_ref, m_ref, l_ref, acc_ref)


def _flash_a(qn, kd, vd, bsz, seq, *, tq=1024, tk=1024):
    n = qn.shape[0]
    tk = min(tk, seq)
    assert seq % tq == 0 and seq % tk == 0
    nq, nk = seq // tq, seq // tk
    grp = A_HEADS // A_KV_HEADS
    qw = grp * HEAD_DIM
    return pl.pallas_call(
        functools.partial(_flash_a_kernel, tq=tq, nk=nk),
        grid=(bsz, A_KV_HEADS, nq, nk + 1),
        in_specs=[
            pl.BlockSpec((tq, qw), lambda b, kv, qi, t: (b * nq + qi, kv)),
            pl.BlockSpec((tk, LANES), lambda b, kv, qi, t: (b * nk + jnp.minimum(t, nk - 1), kv)),
            pl.BlockSpec((tk, LANES), lambda b, kv, qi, t: (b * nk + jnp.maximum(t - 1, 0), kv)),
        ],
        out_specs=pl.BlockSpec((tq, qw), lambda b, kv, qi, t: (b * nq + qi, kv)),
        out_shape=jax.ShapeDtypeStruct((n, A_Q), BF16),
        scratch_shapes=_flash_scratch(tk, grp * tq),
        compiler_params=_cparams(("parallel", "parallel", "parallel", "arbitrary")),
        name="flash_a",
    )(qn, kd, vd)


def _flash_b_kernel(lam_ref, q_ref, k_ref, v_ref, bias_ref, sub_ref, o_ref,
                    qs_ref, s_ref, mb_ref, m_ref, l_ref, acc_ref, *, tq, nk, out_scale):
    qi = pl.program_id(2)
    t = pl.program_id(3)
    lane = lax.broadcasted_iota(jnp.int32, (tq, LANES), 1)
    half0 = lane < HEAD_DIM

    def init():
        x = q_ref[...]
        zero = jnp.zeros_like(x)
        qs_ref[0:tq, :] = jnp.where(half0, x, zero)
        qs_ref[tq:2 * tq, :] = jnp.where(half0, zero, x)

    def scores():
        bias = bias_ref[0, jnp.clip(t - qi, -2, 2) + 2]
        k = k_ref[...]
        for mp in range(2):
            yield mp * tq, lax.dot_general(
                k, qs_ref[mp * tq:(mp + 1) * tq, :], NT_DIMS, preferred_element_type=F32) + bias

    def finish():
        lam = lam_ref[0]
        o_t = acc_ref[...] / jnp.sum(l_ref[...], axis=0, keepdims=True)
        o = (o_t[:, 0:tq] - lam * o_t[:, tq:2 * tq]).T
        ms = jnp.mean(o * o, axis=-1, keepdims=True)
        o_ref[...] = (o * lax.rsqrt(ms + EPS) * sub_ref[...] * out_scale).astype(BF16)

    _flash_steps(t, nk, scores, init, finish, v_ref, s_ref, mb_ref, m_ref, l_ref, acc_ref)


def _flash_b(y, bias_tiles, lam, subln, bsz, seq, lambda_init, *, t):
    n = y.shape[0]
    assert seq % t == 0 and bias_tiles.shape == (B_HEADS, 5, t, t)
    nq = seq // t
    qb, kb, vb = OFF_BQ // LANES, OFF_BK // LANES, OFF_BV // LANES
    return pl.pallas_call(
        functools.partial(_flash_b_kernel, tq=t, nk=nq, out_scale=1.0 - lambda_init),
        grid=(B_HEADS, bsz, nq, nq + 1),
        in_specs=[
            pl.BlockSpec(memory_space=pltpu.SMEM),
            pl.BlockSpec((t, LANES), lambda h, b, qi, ts: (b * nq + qi, qb + h)),
            pl.BlockSpec((t, LANES), lambda h, b, qi, ts: (b * nq + jnp.minimum(ts, nq - 1), kb + h)),
            pl.BlockSpec((t, LANES), lambda h, b, qi, ts: (b * nq + jnp.maximum(ts - 1, 0), vb + h)),
            pl.BlockSpec((1, 5, t, t), lambda h, b, qi, ts: (h, 0, 0, 0), pipeline_mode=pl.Buffered(1)),
            pl.BlockSpec((1, LANES), lambda h, b, qi, ts: (0, 0)),
        ],
        out_specs=pl.BlockSpec((t, LANES), lambda h, b, qi, ts: (b * nq + qi, h)),
        out_shape=jax.ShapeDtypeStruct((n, B_V), BF16),
        scratch_shapes=_flash_scratch(t, 2 * t),
        compiler_params=_cparams(("parallel", "parallel", "parallel", "arbitrary")),
        name="flash_b",
    )(lam, y, y, y, bias_tiles, subln)


def _window_kernel(q_ref, kp_ref, km_ref, kn_ref, vp_ref, vm_ref, vn_ref, bias_ref,
                   o_ref, lse_ref, *, tq, sub, halo, length, dil):
    i = pl.program_id(1)
    nkeys = sub + 2 * halo
    lane = lax.broadcasted_iota(jnp.int32, (sub, LANES), 1)
    half0 = lane < HEAD_DIM
    top_half = lax.broadcasted_iota(jnp.int32, (LANES, sub), 0) < HEAD_DIM
    key_row = lax.broadcasted_iota(jnp.int32, (nkeys, 2 * sub), 0)

    def band(refs, r, a, c):
        cat = jnp.concatenate([ref[0, r, :, c * LANES:(c + 1) * LANES] for ref in refs], axis=0)
        return cat[a * sub:a * sub + nkeys, :]

    def scores(r, a, c, edge):
        qc = q_ref[0, r, a * sub:(a + 1) * sub, c * LANES:(c + 1) * LANES]
        kc = band((kp_ref, km_ref, kn_ref), r, a, c)
        zero = jnp.zeros_like(qc)
        qpair = jnp.concatenate([jnp.where(half0, qc, zero), jnp.where(half0, zero, qc)], axis=0)
        s = lax.dot_general(kc, qpair, NT_DIMS, preferred_element_type=F32) + bias_ref[c]
        if edge:
            ukey = i * tq + a * sub - halo + key_row
            s = jnp.where((ukey >= 0) & (ukey < length), s, NEG_BIG)
        return s

    def softmax(s):
        m = _colmax(s)
        e = jnp.exp2(s - m)
        den = jnp.sum(e.reshape(nkeys // SUBLANES, SUBLANES, 2 * sub), axis=0)
        den = jnp.sum(den, axis=0, keepdims=True)
        return e.astype(BF16), den, (m + jnp.log2(den)) * LN2

    def values(r, a, c, p, den):
        vc = band((vp_ref, vm_ref, vn_ref), r, a, c)
        return lax.dot_general(vc, p, TN_DIMS, preferred_element_type=F32) / den

    def store(r, a, c, o_t, lse):
        rows = pl.ds(a * sub * dil + r, sub, stride=dil) if dil > 1 else pl.ds(a * sub, sub)
        o_ref[c, rows, :] = jnp.where(top_half, o_t[:, 0:sub], o_t[:, sub:2 * sub]).T
        lse_ref[c, rows, :] = jnp.where(top_half, lse[:, 0:sub], lse[:, sub:2 * sub]).T

    def all_residues(edge):
        items = [(r, a, c) for r in range(dil) for a in range(tq // sub) for c in range(C_HEADS // 2)]
        for g0 in range(0, len(items), 8):
            grp = items[g0:g0 + 8]
            ss = [scores(r, a, c, edge) for r, a, c in grp]
            sm = [softmax(s) for s in ss]
            os_ = [values(r, a, c, p, den) for (r, a, c), (p, den, _) in zip(grp, sm)]
            for (r, a, c), o_t, (_, _, lse) in zip(grp, os_, sm):
                store(r, a, c, o_t, lse)

    on_edge = (i == 0) | (i == pl.num_programs(1) - 1)

    @pl.when(on_edge)
    def _():
        all_residues(True)

    @pl.when(jnp.logical_not(on_edge))
    def _():
        all_residues(False)


def _window_group(cg, bias_tile, dil, bsz, seq, *, tq, sub, halo):
    length = seq // dil
    tq = min(tq, length)
    cw = C_OUT
    assert length % tq == 0 and tq % sub == 0 and sub % halo == 0 and length % halo == 0
    assert bias_tile.shape == (C_HEADS // 2, sub + 2 * halo, 2 * sub)
    nt = length // tq
    hb = tq // halo
    nhb = length // halo
    main = lambda col: pl.BlockSpec((1, dil, tq, cw), lambda b, i: (b, 0, i, col))
    prev = lambda col: pl.BlockSpec(
        (1, dil, halo, cw), lambda b, i: (b, 0, jnp.maximum(i * hb - 1, 0), col))
    nxt = lambda col: pl.BlockSpec(
        (1, dil, halo, cw), lambda b, i: (b, 0, jnp.minimum((i + 1) * hb, nhb - 1), col))
    out = pl.BlockSpec((cw // LANES, tq * dil, LANES), lambda b, i: (0, b * nt + i, 0))
    return pl.pallas_call(
        functools.partial(_window_kernel, tq=tq, sub=sub, halo=halo, length=length, dil=dil),
        grid=(bsz, nt),
        in_specs=[main(0), prev(1), main(1), nxt(1), prev(2), main(2), nxt(2),
                  pl.BlockSpec(bias_tile.shape, lambda b, i: (0, 0, 0))],
        out_specs=[out, out],
        out_shape=[jax.ShapeDtypeStruct((cw // LANES, bsz * seq, LANES), F32)] * 2,
        compiler_params=_cparams(("parallel", "parallel")),
        name=f"window_d{dil}",
    )(cg, cg, cg, cg, cg, cg, cg, bias_tile)


def _merge_kernel(x_ref, oa_ref, ob_ref, oc0_ref, oc1_ref, oc2_ref, ls0_ref, ls1_ref, ls2_ref,
                  g0_ref, g1_ref, g2_ref, wb_ref, wo_ref, o_ref):
    planes = lambda ref: jnp.concatenate([ref[c] for c in range(C_OUT // LANES)], axis=1)
    ls = [planes(ls0_ref), planes(ls1_ref), planes(ls2_ref)]
    ocs = [planes(oc0_ref), planes(oc1_ref), planes(oc2_ref)]
    m = jnp.maximum(jnp.maximum(ls[0], ls[1]), ls[2])
    es = [jnp.exp(l - m) for l in ls]
    den = es[0] + es[1] + es[2]
    oc = (es[0] * ocs[0] + es[1] * ocs[1] + es[2] * ocs[2]) / den
    ya = jnp.dot(oa_ref[...], wb_ref[0:A_Q, :], preferred_element_type=F32)
    yb = jnp.dot(ob_ref[...], wb_ref[A_Q:A_Q + B_V, :], preferred_element_type=F32)
    yc = jnp.dot(oc.astype(BF16), wb_ref[A_Q + B_V:, :], preferred_element_type=F32)
    merged = (g0_ref[...].astype(F32) * ya + g1_ref[...].astype(F32) * yb
              + g2_ref[...].astype(F32) * yc)
    o_ref[...] = x_ref[...] + jnp.dot(merged.astype(BF16), wo_ref[...], preferred_element_type=F32)


def _merge(x, oa, ob, ocs, lss, y, wb, wo, *, tm=512):
    n, d = x.shape
    assert n % tm == 0 and OFF_GATE == 0
    row = lambda w: pl.BlockSpec((tm, w), lambda i: (i, 0))
    gate = lambda j: pl.BlockSpec((tm, d), lambda i: (i, j))
    plane = pl.BlockSpec((C_OUT // LANES, tm, LANES), lambda i: (0, i, 0))
    return pl.pallas_call(
        _merge_kernel,
        grid=(n // tm,),
        in_specs=[row(d), row(A_Q), row(B_V)] + [plane] * 6 + [gate(0), gate(1), gate(2)] + [
            pl.BlockSpec(wb.shape, lambda i: (0, 0)),
            pl.BlockSpec(wo.shape, lambda i: (0, 0)),
        ],
        out_specs=row(d),
        out_shape=jax.ShapeDtypeStruct((n, d), F32),
        compiler_params=_cparams(("parallel",)),
        name="merge",
    )(x, oa, ob, *ocs, *lss, y, y, y, wb, wo)


def _rms(x, gain):
    ms = jnp.mean(x * x, axis=-1, keepdims=True)
    return x * lax.rsqrt(ms + EPS) * gain


def _finish(x_ref, acc_ref, o_ref, fg_ref):
    out = x_ref[...] + acc_ref[...]
    if fg_ref is not None:
        out = _rms(out, fg_ref[...])
    o_ref[...] = out


def _ffn_kernel(x_ref, g_ref, w1_ref, w3_ref, w2_ref, *rest, nf, final_norm):
    if final_norm:
        fg_ref, o_ref, h_ref, acc_ref = rest
    else:
        fg_ref = None
        o_ref, h_ref, acc_ref = rest
    j = pl.program_id(1)

    @pl.when(j == 0)
    def _():
        h_ref[...] = _rms(x_ref[...], g_ref[...]).astype(BF16)
        acc_ref[...] = jnp.zeros(acc_ref.shape, F32)

    h = h_ref[...]
    a = jnp.dot(h, w1_ref[...], preferred_element_type=F32)
    b = jnp.dot(h, w3_ref[...], preferred_element_type=F32)
    t = (jax.nn.silu(a) * b).astype(BF16)
    acc_ref[...] += jnp.dot(t, w2_ref[...], preferred_element_type=F32)

    @pl.when(j == nf - 1)
    def _():
        _finish(x_ref, acc_ref, o_ref, fg_ref)


def _ffn(x, gain, w1, w3, w2, final_gain=None, *, tm=512, tf=1408):
    n, d = x.shape
    ff = w1.shape[1]
    assert n % tm == 0 and ff % tf == 0
    nf = ff // tf
    final_norm = final_gain is not None
    in_specs = [
        pl.BlockSpec((tm, d), lambda i, j: (i, 0)),
        pl.BlockSpec((1, d), lambda i, j: (0, 0)),
        pl.BlockSpec((d, tf), lambda i, j: (0, j)),
        pl.BlockSpec((d, tf), lambda i, j: (0, j)),
        pl.BlockSpec((tf, d), lambda i, j: (j, 0)),
    ]
    args = [x, gain.reshape(1, d), w1, w3, w2]
    if final_norm:
        in_specs.append(pl.BlockSpec((1, d), lambda i, j: (0, 0)))
        args.append(final_gain.reshape(1, d))
    return pl.pallas_call(
        functools.partial(_ffn_kernel, nf=nf, final_norm=final_norm),
        grid=(n // tm, nf),
        in_specs=in_specs,
        out_specs=pl.BlockSpec((tm, d), lambda i, j: (i, 0)),
        out_shape=jax.ShapeDtypeStruct((n, d), F32),
        scratch_shapes=[pltpu.VMEM((tm, d), BF16), pltpu.VMEM((tm, d), F32)],
        compiler_params=_cparams(("parallel", "arbitrary")),
        name="ffn",
    )(*args)


def _top2_gate(logits):
    lane = lax.broadcasted_iota(jnp.int32, logits.shape, 1)
    m1 = jnp.max(logits, axis=-1, keepdims=True)
    i1 = jnp.min(jnp.where(logits == m1, lane, LANES), axis=-1, keepdims=True)
    rest = jnp.where(lane == i1, -jnp.inf, logits)
    m2 = jnp.max(rest, axis=-1, keepdims=True)
    i2 = jnp.min(jnp.where(rest == m2, lane, LANES), axis=-1, keepdims=True)
    e2 = jnp.exp(m2 - m1)
    w1 = 1.0 / (1.0 + e2)
    w2 = e2 / (1.0 + e2)
    return jnp.where(lane == i1, w1, 0.0) + jnp.where(lane == i2, w2, 0.0)


def _moe_route_kernel(x_ref, g_ref, r_ref, tri_ref, cum_ref, h_ref, gate_ref, rank_ref, mask_ref, cnt_ref):
    hf = _rms(x_ref[...], g_ref[...])
    h_ref[...] = hf.astype(BF16)
    logits = jnp.dot(hf, r_ref[...], preferred_element_type=F32, precision=lax.Precision.HIGHEST)
    lane = lax.broadcasted_iota(jnp.int32, logits.shape, 1)
    gate = _top2_gate(jnp.where(lane < N_EXPERTS, logits, -jnp.inf))
    gate_ref[...] = gate
    sel = jnp.where(gate.T[0:N_EXPERTS, :] != 0.0, 1.0, 0.0)
    mask_ref[0] = sel
    selb = sel.astype(BF16)
    rank_ref[0] = jnp.dot(selb, tri_ref[...], preferred_element_type=F32)
    cnt_ref[0] = jnp.dot(selb, cum_ref[...], preferred_element_type=F32).astype(jnp.int32)


def _moe_route(x, gain, router_p, tri, cum, *, tm):
    n, d = x.shape
    nt = n // tm
    full = lambda shape: pl.BlockSpec(shape, lambda i: (0,) * len(shape))
    per_tile = lambda shape: pl.BlockSpec((1,) + shape, lambda i: (i, 0, 0))
    return pl.pallas_call(
        _moe_route_kernel,
        grid=(nt,),
        in_specs=[pl.BlockSpec((tm, d), lambda i: (i, 0)), full((1, d)), full((d, LANES)),
                  full((tm, tm)), full((tm, LANES))],
        out_specs=[pl.BlockSpec((tm, d), lambda i: (i, 0)), pl.BlockSpec((tm, LANES), lambda i: (i, 0)),
                   per_tile((N_EXPERTS, tm)), per_tile((N_EXPERTS, tm)), per_tile((N_EXPERTS, LANES))],
        out_shape=[jax.ShapeDtypeStruct((n, d), BF16), jax.ShapeDtypeStruct((n, LANES), F32),
                   jax.ShapeDtypeStruct((nt, N_EXPERTS, tm), F32),
                   jax.ShapeDtypeStruct((nt, N_EXPERTS, tm), F32),
                   jax.ShapeDtypeStruct((nt, N_EXPERTS, LANES), jnp.int32)],
        compiler_params=_cparams(("parallel",)),
        name="moe_route",
    )(x, gain.reshape(1, d), router_p, tri, cum)


def _moe_experts_kernel(cnt_ref, h_ref, gate_ref, rank_ref, mask_ref, w1_ref, w3_ref, w2_ref, o_ref,
                        hc_ref, oc_ref, *, nf, nb, ch, wb, fb):
    i, e, j = pl.program_id(0), pl.program_id(1), pl.program_id(2)
    tm, d = h_ref.shape
    n_rows = cnt_ref[i, e, nb]
    n_chunks = (n_rows + ch - 1) // ch

    def onehot(c, b):
        rank = rank_ref[0, pl.ds(e, 1), b * wb:(b + 1) * wb]
        sel = mask_ref[0, pl.ds(e, 1), b * wb:(b + 1) * wb]
        srow = (c * ch + lax.broadcasted_iota(jnp.int32, (ch, 1), 0)).astype(F32)
        return jnp.where((rank == srow) & (sel > 0.5), 1.0, 0.0).astype(BF16)

    def hits(c, b):
        return (cnt_ref[i, e, b] < (c + 1) * ch) & (cnt_ref[i, e, b + 1] > c * ch)

    @pl.when((e == 0) & (j == 0))
    def _():
        o_ref[...] = jnp.zeros(o_ref.shape, F32)

    @pl.when(j == 0)
    def _():
        def gather(c, carry):
            rows = pl.ds(pl.multiple_of(c * ch, ch), ch)
            hc_ref[rows, :] = jnp.zeros((ch, d), BF16)
            for b in range(nb):
                @pl.when(hits(c, b))
                def _(b=b):
                    picked = jnp.dot(onehot(c, b), h_ref[b * wb:(b + 1) * wb, :], preferred_element_type=F32)
                    hc_ref[rows, :] += picked.astype(BF16)
            return carry
        lax.fori_loop(0, n_chunks, gather, 0)

    def expert(row0, size):
        rows = pl.ds(pl.multiple_of(row0, size), size)
        hcb = hc_ref[rows, :]
        a = jnp.dot(hcb, w1_ref[0], preferred_element_type=F32)
        b3 = jnp.dot(hcb, w3_ref[0], preferred_element_type=F32)
        r = jnp.dot((jax.nn.silu(a) * b3).astype(BF16), w2_ref[0], preferred_element_type=F32)

        @pl.when(j == 0)
        def _():
            oc_ref[rows, :] = r

        @pl.when(j > 0)
        def _():
            oc_ref[rows, :] += r

    def expert_body(f, carry):
        expert(f * fb, fb)
        return carry
    n_full = (n_chunks * ch) // fb
    lax.fori_loop(0, n_full, expert_body, 0)

    @pl.when(n_chunks * ch > n_full * fb)
    def _():
        expert(n_full * fb, ch)

    @pl.when(j == nf - 1)
    def _():
        def scatter(c, carry):
            ocb = oc_ref[pl.ds(pl.multiple_of(c * ch, ch), ch), :].astype(BF16)
            for b in range(nb):
                @pl.when(hits(c, b))
                def _(b=b):
                    back = lax.dot_general(onehot(c, b), ocb, TN_DIMS, preferred_element_type=F32)
                    g = gate_ref[b * wb:(b + 1) * wb, :]
                    lane = lax.broadcasted_iota(jnp.int32, g.shape, 1)
                    gcol = jnp.sum(jnp.where(lane == e, g, 0.0), axis=-1, keepdims=True)
                    o_ref[b * wb:(b + 1) * wb, :] += gcol * back
            return carry
        lax.fori_loop(0, n_chunks, scatter, 0)


def _moe_experts(cnt, h, gate, rank, mask, w1, w3, w2, *, tm, tf=1408, ch=128, wb=256, fb=256):
    n, d = h.shape
    ne, _, ff = w1.shape
    assert n % tm == 0 and ff % tf == 0 and tm % wb == 0 and tm % fb == 0 and fb == 2 * ch
    nf, nb = ff // tf, tm // wb
    assert cnt.shape == (n // tm, ne, nb + 1)
    grid_spec = pltpu.PrefetchScalarGridSpec(
        num_scalar_prefetch=1,
        grid=(n // tm, ne, nf),
        in_specs=[
            pl.BlockSpec((tm, d), lambda i, e, j, c: (i, 0), pipeline_mode=pl.Buffered(1)),
            pl.BlockSpec((tm, LANES), lambda i, e, j, c: (i, 0)),
            pl.BlockSpec((1, ne, tm), lambda i, e, j, c: (i, 0, 0)),
            pl.BlockSpec((1, ne, tm), lambda i, e, j, c: (i, 0, 0)),
            pl.BlockSpec((1, d, tf), lambda i, e, j, c: (e, 0, j)),
            pl.BlockSpec((1, d, tf), lambda i, e, j, c: (e, 0, j)),
            pl.BlockSpec((1, tf, d), lambda i, e, j, c: (e, j, 0)),
        ],
        out_specs=pl.BlockSpec((tm, d), lambda i, e, j, c: (i, 0), pipeline_mode=pl.Buffered(1)),
        scratch_shapes=[pltpu.VMEM((tm, d), BF16), pltpu.VMEM((tm, d), F32)],
    )
    return pl.pallas_call(
        functools.partial(_moe_experts_kernel, nf=nf, nb=nb, ch=ch, wb=wb, fb=fb),
        grid_spec=grid_spec,
        out_shape=jax.ShapeDtypeStruct((n, d), F32),
        compiler_params=_cparams(("parallel", "arbitrary", "arbitrary")),
        name="moe_experts",
    )(cnt, h, gate, rank, mask, w1, w3, w2)


def _residual_kernel(x_ref, y_ref, *rest, final_norm):
    if final_norm:
        fg_ref, o_ref = rest
    else:
        fg_ref = None
        (o_ref,) = rest
    _finish(x_ref, y_ref, o_ref, fg_ref)


def _residual(x, y, final_gain=None, *, tm=1024):
    n, d = x.shape
    assert n % tm == 0
    final_norm = final_gain is not None
    row = pl.BlockSpec((tm, d), lambda i: (i, 0))
    in_specs, args = [row, row], [x, y]
    if final_norm:
        in_specs.append(pl.BlockSpec((1, d), lambda i: (0, 0)))
        args.append(final_gain.reshape(1, d))
    return pl.pallas_call(
        functools.partial(_residual_kernel, final_norm=final_norm),
        grid=(n // tm,),
        in_specs=in_specs,
        out_specs=row,
        out_shape=jax.ShapeDtypeStruct((n, d), F32),
        compiler_params=_cparams(("parallel",)),
        name="residual",
    )(*args)


def _moe(x, gain, router_p, w1, w3, w2, final_gain=None, *, tm=2048, wb=256):
    n = x.shape[0]
    tm = min(tm, n)
    t = jnp.arange(tm)
    tri = (t[:, None] < t[None, :]).astype(BF16)
    cum = (t[:, None] < jnp.arange(LANES)[None, :] * wb).astype(BF16)
    h, gate, rank, mask, cnt = _moe_route(x, gain, router_p, tri, cum, tm=tm)
    y = _moe_experts(cnt[:, :, :tm // wb + 1], h, gate, rank, mask, w1, w3, w2, tm=tm, wb=wb)
    return _residual(x, y, final_gain)


def _t5_bucket(rel):
    nb = NUM_BUCKETS // 2
    max_exact = nb // 2
    ret = jnp.where(rel > 0, nb, 0)
    n = jnp.abs(rel)
    large = max_exact + (jnp.log(jnp.maximum(n, max_exact).astype(F32) / max_exact)
                         / math.log(MAX_DISTANCE / max_exact) * (nb - max_exact)).astype(jnp.int32)
    large = jnp.minimum(large, nb - 1)
    return ret + jnp.where(n < max_exact, n, large)


def _toeplitz_kernel(w_ref, o_ref):
    rows, cols = o_ref.shape[1:]
    w = jnp.broadcast_to(w_ref[0], (rows, w_ref.shape[2]))
    o_ref[0] = pltpu.roll(w, 0, 1, stride=1, stride_axis=0)[:, :cols]


def _toeplitz(value_of_rel, rows, cols, col0):
    period = -(-(rows + cols) // LANES) * LANES
    k = np.arange(period)
    rel = jnp.asarray(col0 + np.where(k < cols, k, k - period), jnp.int32)
    w = value_of_rel(rel).astype(F32)
    lead = w.shape[:-1]
    n = int(np.prod(lead)) if lead else 1
    out = pl.pallas_call(
        _toeplitz_kernel,
        grid=(n,),
        in_specs=[pl.BlockSpec((1, 1, period), lambda i: (i, 0, 0))],
        out_specs=pl.BlockSpec((1, rows, cols), lambda i: (i, 0, 0)),
        out_shape=jax.ShapeDtypeStruct((n, rows, cols), F32),
        compiler_params=_cparams(("parallel",)),
        name="toeplitz",
    )(w.reshape(n, 1, period))
    return out.reshape(lead + (rows, cols))


def _rope_tables(seq):
    t = jnp.arange(seq)
    row = (t // GRID_W).astype(F32)
    col = (t % GRID_W).astype(F32)
    half = HEAD_DIM // 2
    inv = ROPE_THETA ** (-jnp.arange(0, half, 2, dtype=F32) / half)
    ar = row[:, None] * inv
    ac = col[:, None] * inv
    cos = jnp.concatenate([jnp.cos(ar), jnp.cos(ar), jnp.cos(ac), jnp.cos(ac)], axis=1)
    sin = jnp.concatenate([-jnp.sin(ar), jnp.sin(ar), -jnp.sin(ac), jnp.sin(ac)], axis=1)
    return jnp.tile(cos, (1, LANES // HEAD_DIM)), jnp.tile(sin, (1, LANES // HEAD_DIM))


def _bias_tiles_b(rel_bias, t):
    cols = rel_bias[:, :B_HEADS].astype(F32).T * LOG2E
    by_rel = lambda rel: cols[:, _t5_bucket(rel)]
    tiles = []
    for delta in (-2, -1, 0, 1, 2):
        if abs(delta) == 2:
            far = by_rel(jnp.full((1,), delta * t, jnp.int32))
            tiles.append(jnp.broadcast_to(far[:, :, None], (B_HEADS, t, t)))
        else:
            tiles.append(_toeplitz(lambda r: by_rel(-r), t, t, -delta * t))
    return jnp.stack(tiles, axis=1)


def _bias_tile_c(rel_bias, g, dil, span, tq, halo):
    cols = rel_bias[:, B_HEADS + g * C_HEADS:B_HEADS + (g + 1) * C_HEADS].astype(F32).T * LOG2E
    nkeys = tq + 2 * halo

    def by_rel(rel):
        return jnp.where((jnp.abs(rel) <= span)[None, :], cols[:, _t5_bucket(rel * dil)], NEG_BIG)

    per_head = _toeplitz(lambda r: by_rel(-r), nkeys, tq, halo)
    pairs = per_head.reshape(C_HEADS // 2, 2, nkeys, tq).transpose(0, 2, 1, 3)
    return pairs.reshape(C_HEADS // 2, nkeys, 2 * tq)


def _layout_w_in(w):
    a0 = 0
    b0 = a0 + A_SLAB
    c0 = b0 + 2 * B_QK + B_V
    g0 = c0 + 3 * C_QKV
    parts = [w[..., g0:g0 + GATE_COLS], w[..., a0:b0], w[..., b0:c0]]
    for g in range(N_CGROUPS):
        for kind in range(3):
            s = c0 + kind * C_QKV + g * C_OUT
            parts.append(w[..., s:s + C_OUT])
    return jnp.concatenate(parts, axis=-1)


def _trunk(x3, p):
    bsz, seq, d = x3.shape
    n = bsz * seq
    x = x3.reshape(n, d)
    depth = p["ln1"].shape[0]
    assert seq >= MAX_DISTANCE and depth > 0
    cos_t, sin_t = _rope_tables(seq)
    tb = min(1024, seq)
    bias_b = _bias_tiles_b(p["rel_bias"], tb)
    win_halo = 64
    win_sub = 128
    win_tq = [min(512, seq // dil, 4096 // dil) for _, dil in C_GROUPS]
    bias_c = []
    for g, (window, dil) in enumerate(C_GROUPS):
        span = window // (2 * dil)
        assert span <= win_halo
        bias_c.append(_bias_tile_c(p["rel_bias"], g, dil, span, win_sub, win_halo))

    for l in range(depth):
        y, qn, kd, vd, *cgs = _norm_proj(x, p["ln1"][l], p["w_in"][l], p["colscale"], cos_t, sin_t,
                                         p["qg"][l], p["kg"][l], p["bd"], bsz, seq)
        oa = _flash_a(qn, kd, vd, bsz, seq)
        lambda_init = 0.8 - 0.6 * math.exp(-0.3 * l)
        ob = _flash_b(y, bias_b, p["lam"][l], p["subln"][l], bsz, seq, lambda_init, t=tb)
        ocs, lss = [], []
        for g, (window, dil) in enumerate(C_GROUPS):
            oc, ls = _window_group(cgs[g], bias_c[g], dil, bsz, seq, tq=win_tq[g], sub=win_sub,
                                   halo=win_halo)
            ocs.append(oc)
            lss.append(ls)
        x = _merge(x, oa, ob, ocs, lss, y, p["w_br"][l], p["w_o"][l])
        fg = p["ln_f"] if l == depth - 1 else None
        jj = l // 2
        if l % 2 == 0:
            x = _ffn(x, p["ln2"][l], p["ffn_w1"][jj], p["ffn_w3"][jj], p["ffn_w2"][jj], fg)
        else:
            x = _moe(x, p["ln2"][l], p["router"][jj], p["exp_w1"][jj], p["exp_w3"][jj],
                     p["exp_w2"][jj], fg)
    return x.reshape(bsz, seq, d)


def kernel(x_prompt, x_sample, rel_bias, ln1, ln2, ln_f, w_in, a_qnorm, a_knorm, lam_q1, lam_k1,
           lam_q2, lam_k2, b_subln, w_br, w_o, ffn_w1, ffn_w3, ffn_w2, router, exp_w1, exp_w3, exp_w2):
    depth = ln1.shape[0]
    colscale = np.ones((1, IN_COLS), np.float32)
    colscale[:, OFF_BQ:OFF_BQ + B_QK] = QK_SCALE
    for g in range(N_CGROUPS):
        colscale[:, MAIN_COLS + g * C_SLAB:MAIN_COLS + g * C_SLAB + C_OUT] = QK_SCALE
    lam = (jnp.exp(jnp.sum(lam_q1.astype(F32) * lam_k1.astype(F32), axis=-1))
           - jnp.exp(jnp.sum(lam_q2.astype(F32) * lam_k2.astype(F32), axis=-1)))
    lam = lam + jnp.asarray([0.8 - 0.6 * math.exp(-0.3 * l) for l in range(depth)], F32)
    head_id = np.arange(LANES) // HEAD_DIM
    p = {
        "rel_bias": rel_bias,
        "ln1": ln1, "ln2": ln2, "ln_f": ln_f,
        "w_in": _layout_w_in(w_in).astype(BF16),
        "colscale": jnp.asarray(colscale),
        "qg": jnp.tile(a_qnorm.astype(F32), (1, LANES // HEAD_DIM)).reshape(depth, 1, LANES),
        "kg": jnp.tile(a_knorm.astype(F32), (1, LANES // HEAD_DIM)).reshape(depth, 1, LANES),
        "bd": jnp.asarray(head_id[:, None] == head_id[None, :], F32),
        "lam": lam.reshape(depth, 1),
        "subln": b_subln.astype(F32).reshape(depth, 1, 2 * HEAD_DIM),
        "w_br": w_br.astype(BF16), "w_o": w_o.astype(BF16),
        "ffn_w1": ffn_w1.astype(BF16), "ffn_w3": ffn_w3.astype(BF16), "ffn_w2": ffn_w2.astype(BF16),
        "router": jnp.pad(router.astype(F32), ((0, 0), (0, 0), (0, LANES - N_EXPERTS))),
        "exp_w1": exp_w1.astype(BF16), "exp_w3": exp_w3.astype(BF16), "exp_w2": exp_w2.astype(BF16),
    }
    return (_trunk(x_prompt, p), _trunk(x_sample, p))
```

```python
import functools
import math

import jax
import jax.numpy as jnp
import numpy as np
from jax import lax
from jax.experimental import pallas as pl
from jax.experimental.pallas import tpu as pltpu

F32 = jnp.float32
BF16 = jnp.bfloat16

D_MODEL = 1024
HEAD_DIM = 64
GRID_W = 64
ROPE_THETA = 10000.0
EPS = 1e-6
A_HEADS = 8
A_KV_HEADS = 2
B_HEADS = 4
C_GROUPS = ((128, 1), (512, 4), (2048, 16))
C_HEADS = 4
NUM_BUCKETS = 32
MAX_DISTANCE = 128
N_EXPERTS = 8
TOP_K = 2
N_BRANCH = 3

A_Q = A_HEADS * HEAD_DIM
A_KV = A_KV_HEADS * HEAD_DIM
B_QK = B_HEADS * 2 * HEAD_DIM
B_V = B_HEADS * 2 * HEAD_DIM
N_CGROUPS = len(C_GROUPS)
C_QKV = N_CGROUPS * C_HEADS * HEAD_DIM
C_OUT = C_HEADS * HEAD_DIM
GATE_COLS = N_BRANCH * D_MODEL
IN_COLS = A_Q + 2 * A_KV + 2 * B_QK + B_V + 3 * C_QKV + GATE_COLS

OFF_GATE = 0
OFF_AQ = GATE_COLS
A_SLAB = A_Q + 2 * A_KV
OFF_BQ = OFF_AQ + A_SLAB
OFF_BK = OFF_BQ + B_QK
OFF_BV = OFF_BK + B_QK
MAIN_COLS = OFF_BV + B_V
C_SLAB = 3 * C_OUT

LANES = 128
SUBLANES = 8
VMEM_LIMIT = 56 * 1024 * 1024
LOG2E = 1.4426950408889634
LN2 = 0.6931471805599453
QK_SCALE = HEAD_DIM ** -0.5 * LOG2E
NEG_BIG = -1e30

NT_DIMS = (((1,), (1,)), ((), ()))
TN_DIMS = (((0,), (0,)), ((), ()))


def _cparams(sem):
    return pltpu.CompilerParams(dimension_semantics=sem, vmem_limit_bytes=VMEM_LIMIT)


def _split_dot(a, b):
    ah = a.astype(BF16)
    al = (a - ah.astype(F32)).astype(BF16)
    bh = b.astype(BF16)
    bl = (b - bh.astype(F32)).astype(BF16)
    dot = functools.partial(jnp.dot, preferred_element_type=F32)
    return dot(ah, bh) + (dot(ah, bl) + dot(al, bh))


def _prep_a(y, cos, sin, qg, kg, bd, q_out, k_out, v_out):
    tm = y.shape[0]
    lane = lax.broadcasted_iota(jnp.int32, (tm, LANES), 1)
    first16 = (lane % 32) < 16
    half0 = lane < HEAD_DIM

    def norm_rope(x, gain, out_scale):
        ss = _split_dot(x * x, bd)
        xn = x * lax.rsqrt(ss * (1.0 / HEAD_DIM) + EPS) * gain
        sw = jnp.where(first16, pltpu.roll(xn, LANES - 16, 1), pltpu.roll(xn, 16, 1))
        return (xn * cos + sw * sin) * out_scale

    for c in range(A_Q // LANES):
        q_out[:, c * LANES:(c + 1) * LANES] = norm_rope(
            y[:, c * LANES:(c + 1) * LANES], qg, QK_SCALE).astype(BF16)

    k = norm_rope(y[:, A_Q:A_Q + LANES], kg, 1.0)
    kr = pltpu.roll(k, HEAD_DIM, 1)
    k_out[:, 0:LANES] = jnp.where(half0, k, kr).astype(BF16)
    k_out[:, LANES:2 * LANES] = jnp.where(half0, kr, k).astype(BF16)

    v = y[:, A_Q + LANES:A_Q + 2 * LANES]
    vr = pltpu.roll(v, HEAD_DIM, 1)
    v_out[:, 0:LANES] = jnp.where(half0, v, vr).astype(BF16)
    v_out[:, LANES:2 * LANES] = jnp.where(half0, vr, v).astype(BF16)


def _norm_proj_kernel(x_ref, g_ref, w_ref, cs_ref, cos_ref, sin_ref, qg_ref, kg_ref, bd_ref,
                      o_ref, q_ref, k_ref, v_ref, c0_ref, c1_ref, c2_ref, h_ref, ys_ref,
                      *, n_gate_tiles, n_main_tiles, dils):
    j = pl.program_id(1)
    tm = x_ref.shape[0]

    @pl.when(j == 0)
    def _():
        x = x_ref[...]
        ms = jnp.mean(x * x, axis=-1, keepdims=True)
        h_ref[...] = (x * lax.rsqrt(ms + EPS) * g_ref[...]).astype(BF16)

    def project():
        return jnp.dot(h_ref[...], w_ref[...], preferred_element_type=F32) * cs_ref[...]

    @pl.when(j < n_gate_tiles)
    def _():
        o_ref[...] = jax.nn.sigmoid(project()).astype(BF16)

    @pl.when(j == n_gate_tiles)
    def _():
        y = project()
        o_ref[...] = y.astype(BF16)
        _prep_a(y, cos_ref[...], sin_ref[...], qg_ref[...], kg_ref[...], bd_ref[...], q_ref, k_ref, v_ref)

    @pl.when((j > n_gate_tiles) & (j < n_main_tiles))
    def _():
        o_ref[...] = project().astype(BF16)

    for g, (c_ref, dil) in enumerate(zip((c0_ref, c1_ref, c2_ref), dils)):
        @pl.when(j == n_main_tiles + g)
        def _(c_ref=c_ref, dil=dil):
            y = project()
            if dil == 1:
                c_ref[0, 0] = y.astype(BF16)
            else:
                for c in range(y.shape[1] // LANES):
                    ys_ref[c] = y[:, c * LANES:(c + 1) * LANES]
                for r in range(dil):
                    for c in range(y.shape[1] // LANES):
                        c_ref[0, r, :, c * LANES:(c + 1) * LANES] = ys_ref[
                            c, pl.ds(r, tm // dil, stride=dil), :].astype(BF16)


def _norm_proj(x, gain, w, colscale, cos_t, sin_t, qg, kg, bd, bsz, seq, *, tm=1024, tn=C_SLAB):
    n, d = x.shape
    tm = min(tm, seq)
    dils = tuple(dil for _, dil in C_GROUPS)
    assert seq % tm == 0 and tn == C_SLAB and MAIN_COLS % tn == 0 and GATE_COLS % tn == 0
    assert tn == A_SLAB and OFF_AQ == GATE_COLS and A_KV == LANES
    assert all(tm % (2 * SUBLANES * dil) == 0 for dil in dils) and len(dils) == 3
    n_main = MAIN_COLS // tn
    tps = seq // tm

    def c_spec(dil):
        return pl.BlockSpec((1, dil, tm // dil, tn), lambda i, j: (i // tps, 0, i % tps, 0))

    const = lambda shape: pl.BlockSpec(shape, lambda i, j: (0, 0))
    rows = lambda width: pl.BlockSpec((tm, width), lambda i, j: (i, 0))
    return pl.pallas_call(
        functools.partial(_norm_proj_kernel, n_gate_tiles=GATE_COLS // tn, n_main_tiles=n_main, dils=dils),
        grid=(n // tm, n_main + len(dils)),
        in_specs=[
            rows(d),
            const((1, d)),
            pl.BlockSpec((d, tn), lambda i, j: (0, j)),
            pl.BlockSpec((1, tn), lambda i, j: (0, j)),
            pl.BlockSpec((tm, LANES), lambda i, j: (i % tps, 0)),
            pl.BlockSpec((tm, LANES), lambda i, j: (i % tps, 0)),
            const((1, LANES)), const((1, LANES)), const((LANES, LANES)),
        ],
        out_specs=[pl.BlockSpec((tm, tn), lambda i, j: (i, jnp.minimum(j, n_main - 1))),
                   rows(A_Q), rows(2 * LANES), rows(2 * LANES)]
        + [c_spec(dil) for dil in dils],
        out_shape=[jax.ShapeDtypeStruct((n, MAIN_COLS), BF16), jax.ShapeDtypeStruct((n, A_Q), BF16),
                   jax.ShapeDtypeStruct((n, 2 * LANES), BF16), jax.ShapeDtypeStruct((n, 2 * LANES), BF16)]
        + [jax.ShapeDtypeStruct((bsz, dil, seq // dil, tn), BF16) for dil in dils],
        scratch_shapes=[pltpu.VMEM((tm, d), BF16), pltpu.VMEM((tn // LANES, tm, LANES), F32)],
        compiler_params=_cparams(("parallel", "arbitrary")),
        name="norm_proj",
    )(x, gain.reshape(1, d), w, colscale, cos_t, sin_t, qg, kg, bd)


def _colmax(s):
    tk, r = s.shape
    m8 = jnp.max(s.reshape(tk // SUBLANES, SUBLANES, r), axis=0)
    return jnp.max(m8, axis=0, keepdims=True)


def _softmax_pv(s, mb, v_ref, m_ref, l_ref, acc_ref):
    tk, r = s.shape
    m_prev = m_ref[...]
    m_cur = jnp.maximum(m_prev, mb)
    alpha = jnp.exp2(m_prev - m_cur)
    p = jnp.exp2(s - m_cur)
    l_ref[...] = alpha * l_ref[...] + jnp.sum(p.reshape(tk // SUBLANES, SUBLANES, r), axis=0)
    acc_ref[...] = alpha * acc_ref[...] + lax.dot_general(
        v_ref[...], p.astype(BF16), TN_DIMS, preferred_element_type=F32)
    m_ref[...] = m_cur


def _flash_steps(t, nk, scores_fn, init_fn, finish_fn, v_ref, s_ref, mb_ref, m_ref, l_ref, acc_ref):
    def produce(cur):
        for col, s_new in scores_fn():
            width = s_new.shape[1]
            mb_ref[cur, :, col:col + width] = _colmax(s_new)
            s_ref[cur, :, col:col + width] = s_new

    def consume(prev):
        _softmax_pv(s_ref[prev], mb_ref[prev], v_ref, m_ref, l_ref, acc_ref)

    @pl.when(t == 0)
    def _():
        m_ref[...] = jnp.full(m_ref.shape, NEG_BIG, F32)
        l_ref[...] = jnp.zeros(l_ref.shape, F32)
        acc_ref[...] = jnp.zeros(acc_ref.shape, F32)
        init_fn()
        produce(0)

    for par in (0, 1):
        @pl.when((t > 0) & (t < nk) & (t % 2 == par))
        def _(par=par):
            produce(par)
            consume(1 - par)

    @pl.when(t == nk)
    def _():
        consume((nk - 1) % 2)
        finish_fn()


def _flash_scratch(tk, r):
    return [
        pltpu.VMEM((r, LANES), BF16),
        pltpu.VMEM((2, tk, r), F32),
        pltpu.VMEM((2, 1, r), F32),
        pltpu.VMEM((1, r), F32),
        pltpu.VMEM((SUBLANES, r), F32),
        pltpu.VMEM((LANES, r), F32),
    ]


def _flash_a_kernel(q_ref, k_ref, v_ref, o_ref, qs_ref, s_ref, mb_ref, m_ref, l_ref, acc_ref, *, tq, nk):
    t = pl.program_id(3)
    grp = A_HEADS // A_KV_HEADS
    lane = lax.broadcasted_iota(jnp.int32, (tq, LANES), 1)
    half0 = lane < HEAD_DIM

    def init():
        for h in range(grp):
            c = h // 2
            x = q_ref[:, c * LANES:(c + 1) * LANES]
            keep = half0 if h % 2 == 0 else jnp.logical_not(half0)
            qs_ref[h * tq:(h + 1) * tq, :] = jnp.where(keep, x, jnp.zeros_like(x))

    def scores():
        yield 0, lax.dot_general(k_ref[...], qs_ref[...], NT_DIMS, preferred_element_type=F32)

    def finish():
        o_t = acc_ref[...] / jnp.sum(l_ref[...], axis=0, keepdims=True)
        for c in range(grp // 2):
            o0 = o_t[:, (2 * c) * tq:(2 * c + 1) * tq].T
            o1 = o_t[:, (2 * c + 1) * tq:(2 * c + 2) * tq].T
            o_ref[:, c * LANES:(c + 1) * LANES] = jnp.where(half0, o0, o1).astype(BF16)

    _flash_steps(t, nk, scores, init, finish, v_ref, s_ref, mb_ref, m_ref, l_ref, acc_ref)


def _flash_a(qn, kd, vd, bsz, seq, *, tq=1024, tk=1024):
    n = qn.shape[0]
    tk = min(tk, seq)
    assert seq % tq == 0 and seq % tk == 0
    nq, nk = seq // tq, seq // tk
    grp = A_HEADS // A_KV_HEADS
    qw = grp * HEAD_DIM
    return pl.pallas_call(
        functools.partial(_flash_a_kernel, tq=tq, nk=nk),
        grid=(bsz, A_KV_HEADS, nq, nk + 1),
        in_specs=[
            pl.BlockSpec((tq, qw), lambda b, kv, qi, t: (b * nq + qi, kv)),
            pl.BlockSpec((tk, LANES), lambda b, kv, qi, t: (b * nk + jnp.minimum(t, nk - 1), kv)),
            pl.BlockSpec((tk, LANES), lambda b, kv, qi, t: (b * nk + jnp.maximum(t - 1, 0), kv)),
        ],
        out_specs=pl.BlockSpec((tq, qw), lambda b, kv, qi, t: (b * nq + qi, kv)),
        out_shape=jax.ShapeDtypeStruct((n, A_Q), BF16),
        scratch_shapes=_flash_scratch(tk, grp * tq),
        compiler_params=_cparams(("parallel", "parallel", "parallel", "arbitrary")),
        name="flash_a",
    )(qn, kd, vd)


def _flash_b_kernel(lam_ref, q_ref, k_ref, v_ref, bias_ref, sub_ref, o_ref,
                    qs_ref, s_ref, mb_ref, m_ref, l_ref, acc_ref, *, tq, nk, out_scale):
    qi = pl.program_id(2)
    t = pl.program_id(3)
    lane = lax.broadcasted_iota(jnp.int32, (tq, LANES), 1)
    half0 = lane < HEAD_DIM

    def init():
        x = q_ref[...]
        zero = jnp.zeros_like(x)
        qs_ref[0:tq, :] = jnp.where(half0, x, zero)
        qs_ref[tq:2 * tq, :] = jnp.where(half0, zero, x)

    def scores():
        bias = bias_ref[0, jnp.clip(t - qi, -2, 2) + 2]
        k = k_ref[...]
        for mp in range(2):
            yield mp * tq, lax.dot_general(
                k, qs_ref[mp * tq:(mp + 1) * tq, :], NT_DIMS, preferred_element_type=F32) + bias

    def finish():
        lam = lam_ref[0]
        o_t = acc_ref[...] / jnp.sum(l_ref[...], axis=0, keepdims=True)
        o = (o_t[:, 0:tq] - lam * o_t[:, tq:2 * tq]).T
        ms = jnp.mean(o * o, axis=-1, keepdims=True)
        o_ref[...] = (o * lax.rsqrt(ms + EPS) * sub_ref[...] * out_scale).astype(BF16)

    _flash_steps(t, nk, scores, init, finish, v_ref, s_ref, mb_ref, m_ref, l_ref, acc_ref)


def _flash_b(y, bias_tiles, lam, subln, bsz, seq, lambda_init, *, t):
    n = y.shape[0]
    assert seq % t == 0 and bias_tiles.shape == (B_HEADS, 5, t, t)
    nq = seq // t
    qb, kb, vb = OFF_BQ // LANES, OFF_BK // LANES, OFF_BV // LANES
    return pl.pallas_call(
        functools.partial(_flash_b_kernel, tq=t, nk=nq, out_scale=1.0 - lambda_init),
        grid=(B_HEADS, bsz, nq, nq + 1),
        in_specs=[
            pl.BlockSpec(memory_space=pltpu.SMEM),
            pl.BlockSpec((t, LANES), lambda h, b, qi, ts: (b * nq + qi, qb + h)),
            pl.BlockSpec((t, LANES), lambda h, b, qi, ts: (b * nq + jnp.minimum(ts, nq - 1), kb + h)),
            pl.BlockSpec((t, LANES), lambda h, b, qi, ts: (b * nq + jnp.maximum(ts - 1, 0), vb + h)),
            pl.BlockSpec((1, 5, t, t), lambda h, b, qi, ts: (h, 0, 0, 0), pipeline_mode=pl.Buffered(1)),
            pl.BlockSpec((1, LANES), lambda h, b, qi, ts: (0, 0)),
        ],
        out_specs=pl.BlockSpec((t, LANES), lambda h, b, qi, ts: (b * nq + qi, h)),
        out_shape=jax.ShapeDtypeStruct((n, B_V), BF16),
        scratch_shapes=_flash_scratch(t, 2 * t),
        compiler_params=_cparams(("parallel", "parallel", "parallel", "arbitrary")),
        name="flash_b",
    )(lam, y, y, y, bias_tiles, subln)


def _window_kernel(q_ref, kp_ref, km_ref, kn_ref, vp_ref, vm_ref, vn_ref, bias_ref,
                   o_ref, lse_ref, *, tq, sub, halo, length, dil):
    i = pl.program_id(1)
    nkeys = sub + 2 * halo
    lane = lax.broadcasted_iota(jnp.int32, (sub, LANES), 1)
    half0 = lane < HEAD_DIM
    top_half = lax.broadcasted_iota(jnp.int32, (LANES, sub), 0) < HEAD_DIM
    key_row = lax.broadcasted_iota(jnp.int32, (nkeys, 2 * sub), 0)

    def band(refs, r, a, c):
        cat = jnp.concatenate([ref[0, r, :, c * LANES:(c + 1) * LANES] for ref in refs], axis=0)
        return cat[a * sub:a * sub + nkeys, :]

    def scores(r, a, c, edge):
        qc = q_ref[0, r, a * sub:(a + 1) * sub, c * LANES:(c + 1) * LANES]
        kc = band((kp_ref, km_ref, kn_ref), r, a, c)
        zero = jnp.zeros_like(qc)
        qpair = jnp.concatenate([jnp.where(half0, qc, zero), jnp.where(half0, zero, qc)], axis=0)
        s = lax.dot_general(kc, qpair, NT_DIMS, preferred_element_type=F32) + bias_ref[c]
        if edge:
            ukey = i * tq + a * sub - halo + key_row
            s = jnp.where((ukey >= 0) & (ukey < length), s, NEG_BIG)
        return s

    def softmax(s):
        m = _colmax(s)
        e = jnp.exp2(s - m)
        den = jnp.sum(e.reshape(nkeys // SUBLANES, SUBLANES, 2 * sub), axis=0)
        den = jnp.sum(den, axis=0, keepdims=True)
        return e.astype(BF16), den, (m + jnp.log2(den)) * LN2

    def values(r, a, c, p, den):
        vc = band((vp_ref, vm_ref, vn_ref), r, a, c)
        return lax.dot_general(vc, p, TN_DIMS, preferred_element_type=F32) / den

    def store(r, a, c, o_t, lse):
        rows = pl.ds(a * sub * dil + r, sub, stride=dil) if dil > 1 else pl.ds(a * sub, sub)
        o_ref[c, rows, :] = jnp.where(top_half, o_t[:, 0:sub], o_t[:, sub:2 * sub]).T
        lse_ref[c, rows, :] = jnp.where(top_half, lse[:, 0:sub], lse[:, sub:2 * sub]).T

    def all_residues(edge):
        items = [(r, a, c) for r in range(dil) for a in range(tq // sub) for c in range(C_HEADS // 2)]
        for g0 in range(0, len(items), 8):
            grp = items[g0:g0 + 8]
            ss = [scores(r, a, c, edge) for r, a, c in grp]
            sm = [softmax(s) for s in ss]
            os_ = [values(r, a, c, p, den) for (r, a, c), (p, den, _) in zip(grp, sm)]
            for (r, a, c), o_t, (_, _, lse) in zip(grp, os_, sm):
                store(r, a, c, o_t, lse)

    on_edge = (i == 0) | (i == pl.num_programs(1) - 1)

    @pl.when(on_edge)
    def _():
        all_residues(True)

    @pl.when(jnp.logical_not(on_edge))
    def _():
        all_residues(False)


def _window_group(cg, bias_tile, dil, bsz, seq, *, tq, sub, halo):
    length = seq // dil
    tq = min(tq, length)
    cw = C_OUT
    assert length % tq == 0 and tq % sub == 0 and sub % halo == 0 and length % halo == 0
    assert bias_tile.shape == (C_HEADS // 2, sub + 2 * halo, 2 * sub)
    nt = length // tq
    hb = tq // halo
    nhb = length // halo
    main = lambda col: pl.BlockSpec((1, dil, tq, cw), lambda b, i: (b, 0, i, col))
    prev = lambda col: pl.BlockSpec(
        (1, dil, halo, cw), lambda b, i: (b, 0, jnp.maximum(i * hb - 1, 0), col))
    nxt = lambda col: pl.BlockSpec(
        (1, dil, halo, cw), lambda b, i: (b, 0, jnp.minimum((i + 1) * hb, nhb - 1), col))
    out = pl.BlockSpec((cw // LANES, tq * dil, LANES), lambda b, i: (0, b * nt + i, 0))
    return pl.pallas_call(
        functools.partial(_window_kernel, tq=tq, sub=sub, halo=halo, length=length, dil=dil),
        grid=(bsz, nt),
        in_specs=[main(0), prev(1), main(1), nxt(1), prev(2), main(2), nxt(2),
                  pl.BlockSpec(bias_tile.shape, lambda b, i: (0, 0, 0))],
        out_specs=[out, out],
        out_shape=[jax.ShapeDtypeStruct((cw // LANES, bsz * seq, LANES), F32)] * 2,
        compiler_params=_cparams(("parallel", "parallel")),
        name=f"window_d{dil}",
    )(cg, cg, cg, cg, cg, cg, cg, bias_tile)


def _merge_kernel(x_ref, oa_ref, ob_ref, oc0_ref, oc1_ref, oc2_ref, ls0_ref, ls1_ref, ls2_ref,
                  g0_ref, g1_ref, g2_ref, wb_ref, wo_ref, o_ref):
    planes = lambda ref: jnp.concatenate([ref[c] for c in range(C_OUT // LANES)], axis=1)
    ls = [planes(ls0_ref), planes(ls1_ref), planes(ls2_ref)]
    ocs = [planes(oc0_ref), planes(oc1_ref), planes(oc2_ref)]
    m = jnp.maximum(jnp.maximum(ls[0], ls[1]), ls[2])
    es = [jnp.exp(l - m) for l in ls]
    den = es[0] + es[1] + es[2]
    oc = (es[0] * ocs[0] + es[1] * ocs[1] + es[2] * ocs[2]) / den
    ya = jnp.dot(oa_ref[...], wb_ref[0:A_Q, :], preferred_element_type=F32)
    yb = jnp.dot(ob_ref[...], wb_ref[A_Q:A_Q + B_V, :], preferred_element_type=F32)
    yc = jnp.dot(oc.astype(BF16), wb_ref[A_Q + B_V:, :], preferred_element_type=F32)
    merged = (g0_ref[...].astype(F32) * ya + g1_ref[...].astype(F32) * yb
              + g2_ref[...].astype(F32) * yc)
    o_ref[...] = x_ref[...] + jnp.dot(merged.astype(BF16), wo_ref[...], preferred_element_type=F32)


def _merge(x, oa, ob, ocs, lss, y, wb, wo, *, tm=512):
    n, d = x.shape
    assert n % tm == 0 and OFF_GATE == 0
    row = lambda w: pl.BlockSpec((tm, w), lambda i: (i, 0))
    gate = lambda j: pl.BlockSpec((tm, d), lambda i: (i, j))
    plane = pl.BlockSpec((C_OUT // LANES, tm, LANES), lambda i: (0, i, 0))
    return pl.pallas_call(
        _merge_kernel,
        grid=(n // tm,),
        in_specs=[row(d), row(A_Q), row(B_V)] + [plane] * 6 + [gate(0), gate(1), gate(2)] + [
            pl.BlockSpec(wb.shape, lambda i: (0, 0)),
            pl.BlockSpec(wo.shape, lambda i: (0, 0)),
        ],
        out_specs=row(d),
        out_shape=jax.ShapeDtypeStruct((n, d), F32),
        compiler_params=_cparams(("parallel",)),
        name="merge",
    )(x, oa, ob, *ocs, *lss, y, y, y, wb, wo)


def _rms(x, gain):
    ms = jnp.mean(x * x, axis=-1, keepdims=True)
    return x * lax.rsqrt(ms + EPS) * gain


def _finish(x_ref, acc_ref, o_ref, fg_ref):
    out = x_ref[...] + acc_ref[...]
    if fg_ref is not None:
        out = _rms(out, fg_ref[...])
    o_ref[...] = out


def _ffn_kernel(x_ref, g_ref, w1_ref, w3_ref, w2_ref, *rest, nf, final_norm):
    if final_norm:
        fg_ref, o_ref, h_ref, acc_ref = rest
    else:
        fg_ref = None
        o_ref, h_ref, acc_ref = rest
    j = pl.program_id(1)

    @pl.when(j == 0)
    def _():
        h_ref[...] = _rms(x_ref[...], g_ref[...]).astype(BF16)
        acc_ref[...] = jnp.zeros(acc_ref.shape, F32)

    h = h_ref[...]
    a = jnp.dot(h, w1_ref[...], preferred_element_type=F32)
    b = jnp.dot(h, w3_ref[...], preferred_element_type=F32)
    t = (jax.nn.silu(a) * b).astype(BF16)
    acc_ref[...] += jnp.dot(t, w2_ref[...], preferred_element_type=F32)

    @pl.when(j == nf - 1)
    def _():
        _finish(x_ref, acc_ref, o_ref, fg_ref)


def _ffn(x, gain, w1, w3, w2, final_gain=None, *, tm=512, tf=1408):
    n, d = x.shape
    ff = w1.shape[1]
    assert n % tm == 0 and ff % tf == 0
    nf = ff // tf
    final_norm = final_gain is not None
    in_specs = [
        pl.BlockSpec((tm, d), lambda i, j: (i, 0)),
        pl.BlockSpec((1, d), lambda i, j: (0, 0)),
        pl.BlockSpec((d, tf), lambda i, j: (0, j)),
        pl.BlockSpec((d, tf), lambda i, j: (0, j)),
        pl.BlockSpec((tf, d), lambda i, j: (j, 0)),
    ]
    args = [x, gain.reshape(1, d), w1, w3, w2]
    if final_norm:
        in_specs.append(pl.BlockSpec((1, d), lambda i, j: (0, 0)))
        args.append(final_gain.reshape(1, d))
    return pl.pallas_call(
        functools.partial(_ffn_kernel, nf=nf, final_norm=final_norm),
        grid=(n // tm, nf),
        in_specs=in_specs,
        out_specs=pl.BlockSpec((tm, d), lambda i, j: (i, 0)),
        out_shape=jax.ShapeDtypeStruct((n, d), F32),
        scratch_shapes=[pltpu.VMEM((tm, d), BF16), pltpu.VMEM((tm, d), F32)],
        compiler_params=_cparams(("parallel", "arbitrary")),
        name="ffn",
    )(*args)


def _top2_gate(logits):
    lane = lax.broadcasted_iota(jnp.int32, logits.shape, 1)
    m1 = jnp.max(logits, axis=-1, keepdims=True)
    i1 = jnp.min(jnp.where(logits == m1, lane, LANES), axis=-1, keepdims=True)
    rest = jnp.where(lane == i1, -jnp.inf, logits)
    m2 = jnp.max(rest, axis=-1, keepdims=True)
    i2 = jnp.min(jnp.where(rest == m2, lane, LANES), axis=-1, keepdims=True)
    e2 = jnp.exp(m2 - m1)
    w1 = 1.0 / (1.0 + e2)
    w2 = e2 / (1.0 + e2)
    return jnp.where(lane == i1, w1, 0.0) + jnp.where(lane == i2, w2, 0.0)


def _moe_route_kernel(x_ref, g_ref, r_ref, tri_ref, cum_ref, h_ref, gate_ref, rank_ref, mask_ref, cnt_ref):
    hf = _rms(x_ref[...], g_ref[...])
    h_ref[...] = hf.astype(BF16)
    logits = _split_dot(hf, r_ref[...])
    lane = lax.broadcasted_iota(jnp.int32, logits.shape, 1)
    gate = _top2_gate(jnp.where(lane < N_EXPERTS, logits, -jnp.inf))
    gate_ref[...] = gate
    sel = jnp.where(gate.T[0:N_EXPERTS, :] != 0.0, 1.0, 0.0)
    mask_ref[0] = sel
    selb = sel.astype(BF16)
    rank_ref[0] = jnp.dot(selb, tri_ref[...], preferred_element_type=F32)
    cnt_ref[0] = jnp.dot(selb, cum_ref[...], preferred_element_type=F32).astype(jnp.int32)


def _moe_route(x, gain, router_p, tri, cum, *, tm):
    n, d = x.shape
    nt = n // tm
    full = lambda shape: pl.BlockSpec(shape, lambda i: (0,) * len(shape))
    per_tile = lambda shape: pl.BlockSpec((1,) + shape, lambda i: (i, 0, 0))
    return pl.pallas_call(
        _moe_route_kernel,
        grid=(nt,),
        in_specs=[pl.BlockSpec((tm, d), lambda i: (i, 0)), full((1, d)), full((d, LANES)),
                  full((tm, tm)), full((tm, LANES))],
        out_specs=[pl.BlockSpec((tm, d), lambda i: (i, 0)), pl.BlockSpec((tm, LANES), lambda i: (i, 0)),
                   per_tile((N_EXPERTS, tm)), per_tile((N_EXPERTS, tm)), per_tile((N_EXPERTS, LANES))],
        out_shape=[jax.ShapeDtypeStruct((n, d), BF16), jax.ShapeDtypeStruct((n, LANES), F32),
                   jax.ShapeDtypeStruct((nt, N_EXPERTS, tm), F32),
                   jax.ShapeDtypeStruct((nt, N_EXPERTS, tm), F32),
                   jax.ShapeDtypeStruct((nt, N_EXPERTS, LANES), jnp.int32)],
        compiler_params=_cparams(("parallel",)),
        name="moe_route",
    )(x, gain.reshape(1, d), router_p, tri, cum)


def _moe_experts_kernel(cnt_ref, h_ref, gate_ref, rank_ref, mask_ref, w1_ref, w3_ref, w2_ref, o_ref,
                        hc_ref, oc_ref, *, nf, nb, ch, wb, fb):
    i, e, j = pl.program_id(0), pl.program_id(1), pl.program_id(2)
    tm, d = h_ref.shape
    n_rows = cnt_ref[i, e, nb]
    n_chunks = (n_rows + ch - 1) // ch

    def onehot(c, b):
        rank = rank_ref[0, pl.ds(e, 1), b * wb:(b + 1) * wb]
        sel = mask_ref[0, pl.ds(e, 1), b * wb:(b + 1) * wb]
        srow = (c * ch + lax.broadcasted_iota(jnp.int32, (ch, 1), 0)).astype(F32)
        return jnp.where((rank == srow) & (sel > 0.5), 1.0, 0.0).astype(BF16)

    def hits(c, b):
        return (cnt_ref[i, e, b] < (c + 1) * ch) & (cnt_ref[i, e, b + 1] > c * ch)

    @pl.when((e == 0) & (j == 0))
    def _():
        o_ref[...] = jnp.zeros(o_ref.shape, F32)

    @pl.when(j == 0)
    def _():
        def gather(c, carry):
            rows = pl.ds(pl.multiple_of(c * ch, ch), ch)
            hc_ref[rows, :] = jnp.zeros((ch, d), BF16)
            for b in range(nb):
                @pl.when(hits(c, b))
                def _(b=b):
                    picked = jnp.dot(onehot(c, b), h_ref[b * wb:(b + 1) * wb, :], preferred_element_type=F32)
                    hc_ref[rows, :] += picked.astype(BF16)
            return carry
        lax.fori_loop(0, n_chunks, gather, 0)

    def expert(row0, size):
        rows = pl.ds(pl.multiple_of(row0, size), size)
        hcb = hc_ref[rows, :]
        a = jnp.dot(hcb, w1_ref[0], preferred_element_type=F32)
        b3 = jnp.dot(hcb, w3_ref[0], preferred_element_type=F32)
        r = jnp.dot((jax.nn.silu(a) * b3).astype(BF16), w2_ref[0], preferred_element_type=F32)

        @pl.when(j == 0)
        def _():
            oc_ref[rows, :] = r

        @pl.when(j > 0)
        def _():
            oc_ref[rows, :] += r

    def expert_body(f, carry):
        expert(f * fb, fb)
        return carry
    n_full = (n_chunks * ch) // fb
    lax.fori_loop(0, n_full, expert_body, 0)

    @pl.when(n_chunks * ch > n_full * fb)
    def _():
        expert(n_full * fb, ch)

    @pl.when(j == nf - 1)
    def _():
        def scatter(c, carry):
            ocb = oc_ref[pl.ds(pl.multiple_of(c * ch, ch), ch), :].astype(BF16)
            for b in range(nb):
                @pl.when(hits(c, b))
                def _(b=b):
                    back = lax.dot_general(onehot(c, b), ocb, TN_DIMS, preferred_element_type=F32)
                    g = gate_ref[b * wb:(b + 1) * wb, :]
                    lane = lax.broadcasted_iota(jnp.int32, g.shape, 1)
                    gcol = jnp.sum(jnp.where(lane == e, g, 0.0), axis=-1, keepdims=True)
                    o_ref[b * wb:(b + 1) * wb, :] += gcol * back
            return carry
        lax.fori_loop(0, n_chunks, scatter, 0)


def _moe_experts(cnt, h, gate, rank, mask, w1, w3, w2, *, tm, tf=1408, ch=128, wb=256, fb=256):
    n, d = h.shape
    ne, _, ff = w1.shape
    assert n % tm == 0 and ff % tf == 0 and tm % wb == 0 and tm % fb == 0 and fb == 2 * ch
    nf, nb = ff // tf, tm // wb
    assert cnt.shape == (n // tm, ne, nb + 1)
    grid_spec = pltpu.PrefetchScalarGridSpec(
        num_scalar_prefetch=1,
        grid=(n // tm, ne, nf),
        in_specs=[
            pl.BlockSpec((tm, d), lambda i, e, j, c: (i, 0), pipeline_mode=pl.Buffered(1)),
            pl.BlockSpec((tm, LANES), lambda i, e, j, c: (i, 0)),
            pl.BlockSpec((1, ne, tm), lambda i, e, j, c: (i, 0, 0)),
            pl.BlockSpec((1, ne, tm), lambda i, e, j, c: (i, 0, 0)),
            pl.BlockSpec((1, d, tf), lambda i, e, j, c: (e, 0, j)),
            pl.BlockSpec((1, d, tf), lambda i, e, j, c: (e, 0, j)),
            pl.BlockSpec((1, tf, d), lambda i, e, j, c: (e, j, 0)),
        ],
        out_specs=pl.BlockSpec((tm, d), lambda i, e, j, c: (i, 0), pipeline_mode=pl.Buffered(1)),
        scratch_shapes=[pltpu.VMEM((tm, d), BF16), pltpu.VMEM((tm, d), F32)],
    )
    return pl.pallas_call(
        functools.partial(_moe_experts_kernel, nf=nf, nb=nb, ch=ch, wb=wb, fb=fb),
        grid_spec=grid_spec,
        out_shape=jax.ShapeDtypeStruct((n, d), F32),
        compiler_params=_cparams(("parallel", "arbitrary", "arbitrary")),
        name="moe_experts",
    )(cnt, h, gate, rank, mask, w1, w3, w2)


def _residual_kernel(x_ref, y_ref, *rest, final_norm):
    if final_norm:
        fg_ref, o_ref = rest
    else:
        fg_ref = None
        (o_ref,) = rest
    _finish(x_ref, y_ref, o_ref, fg_ref)


def _residual(x, y, final_gain=None, *, tm=1024):
    n, d = x.shape
    assert n % tm == 0
    final_norm = final_gain is not None
    row = pl.BlockSpec((tm, d), lambda i: (i, 0))
    in_specs, args = [row, row], [x, y]
    if final_norm:
        in_specs.append(pl.BlockSpec((1, d), lambda i: (0, 0)))
        args.append(final_gain.reshape(1, d))
    return pl.pallas_call(
        functools.partial(_residual_kernel, final_norm=final_norm),
        grid=(n // tm,),
        in_specs=in_specs,
        out_specs=row,
        out_shape=jax.ShapeDtypeStruct((n, d), F32),
        compiler_params=_cparams(("parallel",)),
        name="residual",
    )(*args)


def _moe(x, gain, router_p, w1, w3, w2, final_gain=None, *, tm=2048, wb=256):
    n = x.shape[0]
    tm = min(tm, n)
    t = jnp.arange(tm)
    tri = (t[:, None] < t[None, :]).astype(BF16)
    cum = (t[:, None] < jnp.arange(LANES)[None, :] * wb).astype(BF16)
    h, gate, rank, mask, cnt = _moe_route(x, gain, router_p, tri, cum, tm=tm)
    y = _moe_experts(cnt[:, :, :tm // wb + 1], h, gate, rank, mask, w1, w3, w2, tm=tm, wb=wb)
    return _residual(x, y, final_gain)


def _t5_bucket(rel):
    nb = NUM_BUCKETS // 2
    max_exact = nb // 2
    ret = jnp.where(rel > 0, nb, 0)
    n = jnp.abs(rel)
    large = max_exact + (jnp.log(jnp.maximum(n, max_exact).astype(F32) / max_exact)
                         / math.log(MAX_DISTANCE / max_exact) * (nb - max_exact)).astype(jnp.int32)
    large = jnp.minimum(large, nb - 1)
    return ret + jnp.where(n < max_exact, n, large)


def _toeplitz_kernel(w_ref, o_ref):
    rows, cols = o_ref.shape[1:]
    w = jnp.broadcast_to(w_ref[0], (rows, w_ref.shape[2]))
    o_ref[0] = pltpu.roll(w, 0, 1, stride=1, stride_axis=0)[:, :cols]


def _toeplitz(value_of_rel, rows, cols, col0):
    period = -(-(rows + cols) // LANES) * LANES
    k = np.arange(period)
    rel = jnp.asarray(col0 + np.where(k < cols, k, k - period), jnp.int32)
    w = value_of_rel(rel).astype(F32)
    lead = w.shape[:-1]
    n = int(np.prod(lead)) if lead else 1
    out = pl.pallas_call(
        _toeplitz_kernel,
        grid=(n,),
        in_specs=[pl.BlockSpec((1, 1, period), lambda i: (i, 0, 0))],
        out_specs=pl.BlockSpec((1, rows, cols), lambda i: (i, 0, 0)),
        out_shape=jax.ShapeDtypeStruct((n, rows, cols), F32),
        compiler_params=_cparams(("parallel",)),
        name="toeplitz",
    )(w.reshape(n, 1, period))
    return out.reshape(lead + (rows, cols))


def _rope_tables(seq):
    t = jnp.arange(seq)
    row = (t // GRID_W).astype(F32)
    col = (t % GRID_W).astype(F32)
    half = HEAD_DIM // 2
    inv = ROPE_THETA ** (-jnp.arange(0, half, 2, dtype=F32) / half)
    ar = row[:, None] * inv
    ac = col[:, None] * inv
    cos = jnp.concatenate([jnp.cos(ar), jnp.cos(ar), jnp.cos(ac), jnp.cos(ac)], axis=1)
    sin = jnp.concatenate([-jnp.sin(ar), jnp.sin(ar), -jnp.sin(ac), jnp.sin(ac)], axis=1)
    return jnp.tile(cos, (1, LANES // HEAD_DIM)), jnp.tile(sin, (1, LANES // HEAD_DIM))


def _bias_tiles_b(rel_bias, t):
    cols = rel_bias[:, :B_HEADS].astype(F32).T * LOG2E
    by_rel = lambda rel: cols[:, _t5_bucket(rel)]
    tiles = []
    for delta in (-2, -1, 0, 1, 2):
        if abs(delta) == 2:
            far = by_rel(jnp.full((1,), delta * t, jnp.int32))
            tiles.append(jnp.broadcast_to(far[:, :, None], (B_HEADS, t, t)))
        else:
            tiles.append(_toeplitz(lambda r: by_rel(-r), t, t, -delta * t))
    return jnp.stack(tiles, axis=1)


def _bias_tile_c(rel_bias, g, dil, span, tq, halo):
    cols = rel_bias[:, B_HEADS + g * C_HEADS:B_HEADS + (g + 1) * C_HEADS].astype(F32).T * LOG2E
    nkeys = tq + 2 * halo

    def by_rel(rel):
        return jnp.where((jnp.abs(rel) <= span)[None, :], cols[:, _t5_bucket(rel * dil)], NEG_BIG)

    per_head = _toeplitz(lambda r: by_rel(-r), nkeys, tq, halo)
    pairs = per_head.reshape(C_HEADS // 2, 2, nkeys, tq).transpose(0, 2, 1, 3)
    return pairs.reshape(C_HEADS // 2, nkeys, 2 * tq)


def _layout_w_in(w):
    a0 = 0
    b0 = a0 + A_SLAB
    c0 = b0 + 2 * B_QK + B_V
    g0 = c0 + 3 * C_QKV
    parts = [w[..., g0:g0 + GATE_COLS], w[..., a0:b0], w[..., b0:c0]]
    for g in range(N_CGROUPS):
        for kind in range(3):
            s = c0 + kind * C_QKV + g * C_OUT
            parts.append(w[..., s:s + C_OUT])
    return jnp.concatenate(parts, axis=-1)


def _trunk(x3, p):
    bsz, seq, d = x3.shape
    n = bsz * seq
    x = x3.reshape(n, d)
    depth = p["ln1"].shape[0]
    assert seq >= MAX_DISTANCE and depth > 0
    cos_t, sin_t = _rope_tables(seq)
    tb = min(1024, seq)
    bias_b = _bias_tiles_b(p["rel_bias"], tb)
    win_halo = 64
    win_sub = 128
    win_tq = [min(512, seq // dil, 4096 // dil) for _, dil in C_GROUPS]
    bias_c = []
    for g, (window, dil) in enumerate(C_GROUPS):
        span = window // (2 * dil)
        assert span <= win_halo
        bias_c.append(_bias_tile_c(p["rel_bias"], g, dil, span, win_sub, win_halo))

    for l in range(depth):
        y, qn, kd, vd, *cgs = _norm_proj(x, p["ln1"][l], p["w_in"][l], p["colscale"], cos_t, sin_t,
                                         p["qg"][l], p["kg"][l], p["bd"], bsz, seq)
        oa = _flash_a(qn, kd, vd, bsz, seq)
        lambda_init = 0.8 - 0.6 * math.exp(-0.3 * l)
        ob = _flash_b(y, bias_b, p["lam"][l], p["subln"][l], bsz, seq, lambda_init, t=tb)
        ocs, lss = [], []
        for g, (window, dil) in enumerate(C_GROUPS):
            oc, ls = _window_group(cgs[g], bias_c[g], dil, bsz, seq, tq=win_tq[g], sub=win_sub,
                                   halo=win_halo)
            ocs.append(oc)
            lss.append(ls)
        x = _merge(x, oa, ob, ocs, lss, y, p["w_br"][l], p["w_o"][l])
        fg = p["ln_f"] if l == depth - 1 else None
        jj = l // 2
        if l % 2 == 0:
            x = _ffn(x, p["ln2"][l], p["ffn_w1"][jj], p["ffn_w3"][jj], p["ffn_w2"][jj], fg)
        else:
            x = _moe(x, p["ln2"][l], p["router"][jj], p["exp_w1"][jj], p["exp_w3"][jj],
                     p["exp_w2"][jj], fg)
    return x.reshape(bsz, seq, d)


def kernel(x_prompt, x_sample, rel_bias, ln1, ln2, ln_f, w_in, a_qnorm, a_knorm, lam_q1, lam_k1,
           lam_q2, lam_k2, b_subln, w_br, w_o, ffn_w1, ffn_w3, ffn_w2, router, exp_w1, exp_w3, exp_w2):
    depth = ln1.shape[0]
    colscale = np.ones((1, IN_COLS), np.float32)
    colscale[:, OFF_BQ:OFF_BQ + B_QK] = QK_SCALE
    for g in range(N_CGROUPS):
        colscale[:, MAIN_COLS + g * C_SLAB:MAIN_COLS + g * C_SLAB + C_OUT] = QK_SCALE
    lam = (jnp.exp(jnp.sum(lam_q1.astype(F32) * lam_k1.astype(F32), axis=-1))
           - jnp.exp(jnp.sum(lam_q2.astype(F32) * lam_k2.astype(F32), axis=-1)))
    lam = lam + jnp.asarray([0.8 - 0.6 * math.exp(-0.3 * l) for l in range(depth)], F32)
    head_id = np.arange(LANES) // HEAD_DIM
    p = {
        "rel_bias": rel_bias,
        "ln1": ln1, "ln2": ln2, "ln_f": ln_f,
        "w_in": _layout_w_in(w_in).astype(BF16),
        "colscale": jnp.asarray(colscale),
        "qg": jnp.tile(a_qnorm.astype(F32), (1, LANES // HEAD_DIM)).reshape(depth, 1, LANES),
        "kg": jnp.tile(a_knorm.astype(F32), (1, LANES // HEAD_DIM)).reshape(depth, 1, LANES),
        "bd": jnp.asarray(head_id[:, None] == head_id[None, :], F32),
        "lam": lam.reshape(depth, 1),
        "subln": b_subln.astype(F32).reshape(depth, 1, 2 * HEAD_DIM),
        "w_br": w_br.astype(BF16), "w_o": w_o.astype(BF16),
        "ffn_w1": ffn_w1.astype(BF16), "ffn_w3": ffn_w3.astype(BF16), "ffn_w2": ffn_w2.astype(BF16),
        "router": jnp.pad(router.astype(F32), ((0, 0), (0, 0), (0, LANES - N_EXPERTS))),
        "exp_w1": exp_w1.astype(BF16), "exp_w3": exp_w3.astype(BF16), "exp_w2": exp_w2.astype(BF16),
    }
    return (_trunk(x_prompt, p), _trunk(x_sample, p))
```
